```python
import jax, jax.numpy as jnp
from jax import lax
import numpy as np

D_MODEL = 1024
BATCH = 2
SEQ = 16384
DEPTH = 4

CHUNK = 64
N_MIXERS = 2
N_LAYERS_A = (DEPTH + 1) // 2
N_LAYERS_B = DEPTH // 2
SGU_BLOCK = 128
SGU_WIDTH = 2 * D_MODEL
SGU_GROUPS = 16
SGU_GROUP_DIM = SGU_WIDTH // SGU_GROUPS
RWKV_HEAD_DIM = 64
RWKV_HEADS = D_MODEL // RWKV_HEAD_DIM
DECAY_LORA = 64
AAA_LORA = 64
MV_LORA = 32
GATE_LORA = 128
FFN_HIDDEN = 4 * D_MODEL
RMS_EPS = 1e-6
LN_EPS = 1e-5
GN_EPS = 64e-5

kernel_name = "hybrid_sgu_rwkv7_trunk"


def _rmsnorm(x, g):
    xf = x.astype(jnp.float32)
    y = xf * lax.rsqrt(jnp.mean(xf * xf, axis=-1, keepdims=True) + RMS_EPS)
    return (y * g.astype(jnp.float32)).astype(x.dtype)


def _layernorm(x, g, b, eps):
    xf = x.astype(jnp.float32)
    mu = jnp.mean(xf, axis=-1, keepdims=True)
    var = jnp.mean(jnp.square(xf - mu), axis=-1, keepdims=True)
    y = (xf - mu) * lax.rsqrt(var + eps)
    return (y * g.astype(jnp.float32) + b.astype(jnp.float32)).astype(x.dtype)


def _sgu_mask():
    chunk_id = jnp.arange(SGU_BLOCK) // CHUNK
    return chunk_id[None, :] <= chunk_id[:, None]


def _spatial_gating_mixer(h, w_in, b_in, ln_g, ln_b, w_s, b_s, w_out):
    bsz, seq, _ = h.shape
    z = jax.nn.gelu(h @ w_in + b_in, approximate=False)
    u, v = jnp.split(z, 2, axis=-1)
    v = _layernorm(v, ln_g, ln_b, LN_EPS)
    v = v.reshape(bsz, seq // SGU_BLOCK, SGU_BLOCK, SGU_GROUPS, SGU_GROUP_DIM)
    w_s = w_s * _sgu_mask()[None].astype(w_s.dtype)
    s = jnp.einsum('gij,bnjgc->bnigc', w_s, v) + b_s.T[None, None, :, :, None]
    y = u * s.reshape(bsz, seq, SGU_WIDTH)
    return y @ w_out


def _wkv7_scan(r, w, k, v, kk, kka):
    bsz, _, nh, n = r.shape
    xs = tuple(jnp.moveaxis(t, 1, 0) for t in (r, w, k, v, kk, kka))

    def step(S, inp):
        r_t, w_t, k_t, v_t, kk_t, kka_t = inp
        sa = jnp.einsum('bhvk,bhk->bhv', S, -kk_t)
        S = (S * w_t[:, :, None, :] + sa[..., None] * kka_t[:, :, None, :]
             + v_t[..., None] * k_t[:, :, None, :])
        y = jnp.einsum('bhvk,bhk->bhv', S, r_t)
        return S, y

    S0 = jnp.zeros((bsz, nh, n, n), jnp.float32)
    _, y = lax.scan(step, S0, xs)
    return jnp.moveaxis(y, 0, 1)


def _rwkv7_mixer(h, mu, w_rkv, w0, w_la, w_lb, a0, a_la, a_lb, g_la, g_lb,
                 k_k, k_a, r_k, ln_g, ln_b, w_o, v_first, v_mix):
    bsz, seq, d = h.shape
    f32 = jnp.float32

    def heads(t):
        return t.reshape(bsz, seq, RWKV_HEADS, RWKV_HEAD_DIM)

    xx = jnp.pad(h, ((0, 0), (1, 0), (0, 0)))[:, :-1] - h
    xs = h[None] + xx[None] * mu[:, None, None, :]
    rkv = jnp.einsum('pbtd,pde->pbte', xs[:3], w_rkv)
    r, k, v = rkv[0], rkv[1], rkv[2]
    xv, xw, xa, xg = xs[2], xs[3], xs[4], xs[5]
    log_w = -jax.nn.softplus(-(w0 + jnp.tanh(xw @ w_la) @ w_lb)) - 0.5
    decay = jnp.exp(-jnp.exp(log_w.astype(f32)))
    if v_mix is None:
        v_first = v
    else:
        v0, v_la, v_lb = v_mix
        v = v + (v_first - v) * jax.nn.sigmoid(v0 + (xv @ v_la) @ v_lb)
    a = jax.nn.sigmoid(a0 + (xa @ a_la) @ a_lb)
    g = jax.nn.sigmoid(xg @ g_la) @ g_lb
    kk = heads(k * k_k).astype(f32)
    kk = kk / jnp.maximum(jnp.sqrt(jnp.sum(kk * kk, axis=-1, keepdims=True)), 1e-12)
    k = k * (1.0 + (a - 1.0) * k_a)
    r_h, k_h, v_h = heads(r).astype(f32), heads(k).astype(f32), heads(v).astype(f32)
    y = _wkv7_scan(r_h, heads(decay), k_h, v_h, kk, kk * heads(a).astype(f32))
    mu_y = jnp.mean(y, axis=-1, keepdims=True)
    var_y = jnp.mean(jnp.square(y - mu_y), axis=-1, keepdims=True)
    y = ((y - mu_y) * lax.rsqrt(var_y + GN_EPS)).reshape(bsz, seq, d)
    y = y * ln_g.astype(f32) + ln_b.astype(f32)
    bonus = jnp.sum(r_h * k_h * r_k.astype(f32), axis=-1, keepdims=True) * v_h
    y = (y + bonus.reshape(bsz, seq, d)) * g.astype(f32)
    return y.astype(h.dtype) @ w_o, v_first


def _squared_relu_mlp(h, w1, w2):
    return jnp.square(jax.nn.relu(h @ w1)) @ w2


def setup_inputs(seed: int = 0) -> dict:
    key = jax.random.key(seed)
    ks = jax.random.split(key, 32)
    f32 = jnp.float32
    D, E, G, H, N = D_MODEL, SGU_WIDTH, SGU_GROUPS, RWKV_HEADS, RWKV_HEAD_DIM
    nA, nB = N_LAYERS_A, N_LAYERS_B

    def nrm(k, shape, scale):
        return jax.random.normal(k, shape, f32) * scale

    return {
        "x": nrm(ks[0], (BATCH, SEQ, D), 1.0),
        "norm_mix_g": 1.0 + nrm(ks[1], (DEPTH, D), 0.05),
        "norm_ffn_g": 1.0 + nrm(ks[2], (DEPTH, D), 0.05),
        "final_norm_g": 1.0 + nrm(ks[3], (D,), 0.05),
        "ffn_w1": nrm(ks[4], (DEPTH, D, FFN_HIDDEN), D ** -0.5),
        "ffn_w2": nrm(ks[5], (DEPTH, FFN_HIDDEN, D), 0.5 * FFN_HIDDEN ** -0.5),
        "sgu_w_in": nrm(ks[6], (nA, D, 2 * E), D ** -0.5),
        "sgu_b_in": nrm(ks[7], (nA, 2 * E), 0.02),
        "sgu_ln_g": 1.0 + nrm(ks[8], (nA, E), 0.05),
        "sgu_ln_b": nrm(ks[9], (nA, E), 0.02),
        "sgu_w_s": nrm(ks[10], (nA, G, SGU_BLOCK, SGU_BLOCK), SGU_BLOCK ** -0.5),
        "sgu_b_s": 1.0 + nrm(ks[11], (nA, G, SGU_BLOCK), 0.1),
        "sgu_w_out": nrm(ks[12], (nA, E, D), 0.5 * E ** -0.5),
        "rwkv_mu": jax.random.uniform(ks[13], (nB, 6, D), f32),
        "rwkv_w_rkv": nrm(ks[14], (nB, 3, D, D), D ** -0.5),
        "rwkv_w0": jax.random.uniform(ks[15], (nB, D), f32, -6.0, 0.5),
        "rwkv_w_lora_a": nrm(ks[16], (nB, D, DECAY_LORA), D ** -0.5),
        "rwkv_w_lora_b": nrm(ks[17], (nB, DECAY_LORA, D), 0.3 * DECAY_LORA ** -0.5),
        "rwkv_a0": nrm(ks[18], (nB, D), 0.1),
        "rwkv_a_lora_a": nrm(ks[19], (nB, D, AAA_LORA), D ** -0.5),
        "rwkv_a_lora_b": nrm(ks[20], (nB, AAA_LORA, D), 0.3 * AAA_LORA ** -0.5),
        "rwkv_v0": 0.5 + nrm(ks[21], (nB - 1, D), 0.1),
        "rwkv_v_lora_a": nrm(ks[22], (nB - 1, D, MV_LORA), D ** -0.5),
        "rwkv_v_lora_b": nrm(ks[23], (nB - 1, MV_LORA, D), 0.3 * MV_LORA ** -0.5),
        "rwkv_g_lora_a": nrm(ks[24], (nB, D, GATE_LORA), D ** -0.5),
        "rwkv_g_lora_b": nrm(ks[25], (nB, GATE_LORA, D), GATE_LORA ** -0.5),
        "rwkv_k_k": 0.85 + nrm(ks[26], (nB, D), 0.05),
        "rwkv_k_a": 1.0 + nrm(ks[27], (nB, D), 0.05),
        "rwkv_r_k": nrm(ks[28], (nB, H, N), 0.1),
        "rwkv_ln_g": 1.0 + nrm(ks[29], (nB, D), 0.05),
        "rwkv_ln_b": nrm(ks[30], (nB, D), 0.02),
        "rwkv_w_o": nrm(ks[31], (nB, D, D), 0.5 * D ** -0.5),
    }


def reference(x, norm_mix_g, norm_ffn_g, final_norm_g, ffn_w1, ffn_w2,
              sgu_w_in, sgu_b_in, sgu_ln_g, sgu_ln_b, sgu_w_s, sgu_b_s, sgu_w_out,
              rwkv_mu, rwkv_w_rkv, rwkv_w0, rwkv_w_lora_a, rwkv_w_lora_b,
              rwkv_a0, rwkv_a_lora_a, rwkv_a_lora_b,
              rwkv_v0, rwkv_v_lora_a, rwkv_v_lora_b,
              rwkv_g_lora_a, rwkv_g_lora_b, rwkv_k_k, rwkv_k_a, rwkv_r_k,
              rwkv_ln_g, rwkv_ln_b, rwkv_w_o):
    v_first = None
    for i in range(DEPTH):
        h = _rmsnorm(x, norm_mix_g[i])
        j = i // N_MIXERS
        if i % N_MIXERS == 0:
            y = _spatial_gating_mixer(h, sgu_w_in[j], sgu_b_in[j], sgu_ln_g[j], sgu_ln_b[j],
                                      sgu_w_s[j], sgu_b_s[j], sgu_w_out[j])
        else:
            v_mix = None if j == 0 else (rwkv_v0[j - 1], rwkv_v_lora_a[j - 1], rwkv_v_lora_b[j - 1])
            y, v_first = _rwkv7_mixer(h, rwkv_mu[j], rwkv_w_rkv[j], rwkv_w0[j],
                                      rwkv_w_lora_a[j], rwkv_w_lora_b[j],
                                      rwkv_a0[j], rwkv_a_lora_a[j], rwkv_a_lora_b[j],
                                      rwkv_g_lora_a[j], rwkv_g_lora_b[j],
                                      rwkv_k_k[j], rwkv_k_a[j], rwkv_r_k[j],
                                      rwkv_ln_g[j], rwkv_ln_b[j], rwkv_w_o[j],
                                      v_first, v_mix)
        x = x + y
        x = x + _squared_relu_mlp(_rmsnorm(x, norm_ffn_g[i]), ffn_w1[i], ffn_w2[i])
    return _rmsnorm(x, final_norm_g)
```

```python
import functools

import jax
import jax.numpy as jnp
from jax import lax
from jax.experimental import pallas as pl
from jax.experimental.pallas import tpu as pltpu

F32 = jnp.float32
BF16 = jnp.bfloat16

D_MODEL = 1024
DEPTH = 4
CHUNK = 64
SGU_BLOCK = 128
SGU_WIDTH = 2 * D_MODEL
SGU_GROUPS = 16
SGU_GROUP_DIM = SGU_WIDTH // SGU_GROUPS
HEAD_DIM = 64
HEADS = D_MODEL // HEAD_DIM
FFN_HIDDEN = 4 * D_MODEL
RMS_EPS = 1e-6
LN_EPS = 1e-5
GN_EPS = 64e-5

LANES = 128
WKV_CHUNK = 64
PAIR = 2 * HEAD_DIM
WKV_GROUP = 16

TM_FFN = 512
TM_SGU = 256
TM_RWKV = 256
VMEM_LIMIT = 56 * 1024 * 1024


def _dot(a, b):
    return jnp.dot(a, b, preferred_element_type=F32)


def _dot_nt(a, b):
    return lax.dot_general(a, b, (((1,), (1,)), ((), ())), preferred_element_type=F32)


def _dot_tn(a, b):
    return lax.dot_general(a, b, (((0,), (0,)), ((), ())), preferred_element_type=F32)


def _split3(x):
    hi = x.astype(BF16)
    r1 = x - hi.astype(F32)
    mid = r1.astype(BF16)
    lo = (r1 - mid.astype(F32)).astype(BF16)
    return hi, mid, lo


def _rms(x, g):
    return x * lax.rsqrt(jnp.mean(x * x, axis=-1, keepdims=True) + RMS_EPS) * g


def _gelu(x):
    return 0.5 * x * (1.0 + lax.erf(x * 0.7071067811865476))


def _sigmoid(x):
    return 1.0 / (1.0 + jnp.exp(-x))


def _const_spec(shape):
    nd = len(shape)
    return pl.BlockSpec(shape, lambda i: (0,) * nd, pipeline_mode=pl.Buffered(1))


def _row_spec(tm, width=D_MODEL):
    return pl.BlockSpec((tm, width), lambda i: (i, 0))


def _params():
    return pltpu.CompilerParams(dimension_semantics=("arbitrary",), vmem_limit_bytes=VMEM_LIMIT)


def _ffn_body(x, g, w1_ref, w2_ref):
    xn = _rms(x, g).astype(BF16)
    acc = x
    for c in range(FFN_HIDDEN // D_MODEL):
        sl = slice(c * D_MODEL, (c + 1) * D_MODEL)
        h = jnp.maximum(_dot(xn, w1_ref[:, sl]), 0.0)
        acc = acc + _dot((h * h).astype(BF16), w2_ref[sl, :])
    return acc


def _ffn_kernel(x_ref, g_ref, w1_ref, w2_ref, o_ref):
    o_ref[...] = _ffn_body(x_ref[...], g_ref[...], w1_ref, w2_ref)


def _ffn(x, g, w1, w2):
    m = x.shape[0]
    return pl.pallas_call(
        _ffn_kernel,
        grid=(m // TM_FFN,),
        in_specs=[_row_spec(TM_FFN), _const_spec((1, D_MODEL)), _const_spec((D_MODEL, FFN_HIDDEN)),
                  _const_spec((FFN_HIDDEN, D_MODEL))],
        out_specs=_row_spec(TM_FFN),
        out_shape=jax.ShapeDtypeStruct((m, D_MODEL), F32),
        compiler_params=_params(),
        name="ffn",
    )(x, g, w1, w2)


def _sgu_kernel(x_ref, g_ref, win_ref, bin_ref, lng_ref, lnb_ref, ws_ref, bs_ref, wout_ref, o_ref,
                vn_ref, y_ref):
    x = x_ref[...]
    xn = _rms(x, g_ref[...]).astype(BF16)
    e = SGU_WIDTH
    zv = _gelu(_dot(xn, win_ref[:, e:]) + bin_ref[:, e:])
    mu = jnp.mean(zv, axis=-1, keepdims=True)
    zc = zv - mu
    var = jnp.mean(zc * zc, axis=-1, keepdims=True)
    vn_ref[...] = (zc * lax.rsqrt(var + LN_EPS) * lng_ref[...] + lnb_ref[...]).astype(BF16)
    nblk = TM_SGU // SGU_BLOCK
    qi = lax.broadcasted_iota(jnp.int32, (SGU_BLOCK, SGU_BLOCK), 0) // CHUNK
    kj = lax.broadcasted_iota(jnp.int32, (SGU_BLOCK, SGU_BLOCK), 1) // CHUNK
    causal = kj <= qi
    for gi in range(SGU_GROUPS):
        cs = slice(gi * SGU_GROUP_DIM, (gi + 1) * SGU_GROUP_DIM)
        ws = jnp.where(causal, ws_ref[gi], 0.0).astype(BF16)
        rhs = jnp.concatenate([vn_ref[n * SGU_BLOCK:(n + 1) * SGU_BLOCK, cs] for n in range(nblk)], axis=1)
        s = _dot(ws, rhs) + bs_ref[gi]
        u = _gelu(_dot(xn, win_ref[:, cs]) + bin_ref[:, cs])
        for n in range(nblk):
            rs = slice(n * SGU_BLOCK, (n + 1) * SGU_BLOCK)
            y_ref[rs, cs] = (u[rs, :] * s[:, n * SGU_GROUP_DIM:(n + 1) * SGU_GROUP_DIM]).astype(BF16)
    o_ref[...] = x + _dot(y_ref[...], wout_ref[...])


def _sgu(x, g, w_in, b_in, ln_g, ln_b, w_s, b_s, w_out):
    m = x.shape[0]
    e = SGU_WIDTH
    return pl.pallas_call(
        _sgu_kernel,
        grid=(m // TM_SGU,),
        in_specs=[_row_spec(TM_SGU), _const_spec((1, D_MODEL)), _const_spec((D_MODEL, 2 * e)),
                  _const_spec((1, 2 * e)), _const_spec((1, e)), _const_spec((1, e)),
                  _const_spec((SGU_GROUPS, SGU_BLOCK, SGU_BLOCK)), _const_spec((SGU_GROUPS, SGU_BLOCK, 1)),
                  _const_spec((e, D_MODEL))],
        out_specs=_row_spec(TM_SGU),
        out_shape=jax.ShapeDtypeStruct((m, D_MODEL), F32),
        scratch_shapes=[pltpu.VMEM((TM_SGU, e), BF16), pltpu.VMEM((TM_SGU, e), BF16)],
        compiler_params=_params(),
        name="sgu",
    )(x, g, w_in, b_in, ln_g, ln_b, w_s, b_s, w_out)


def _head_sum(x, ind_ref, indt_ref):
    s = _dot(x.astype(BF16), ind_ref[...])
    hi = s.astype(BF16)
    lo = (s - hi.astype(F32)).astype(BF16)
    return _dot(jnp.concatenate([hi, lo], axis=1), indt_ref[...])


def _rwkv_in_kernel(*refs, tiles_per_seq, has_vmix):
    if has_vmix:
        (x_ref, xp_ref, vf_ref, g_ref, mu_ref, wrkv_ref, w0_ref, wla_ref, wlb_ref, a0_ref, ala_ref, alb_ref,
         v0_ref, vla_ref, vlb_ref, gla_ref, glb_ref, kk_ref, ka_ref, rk_ref, ind_ref, indt_ref,
         r_o, w_o, k_o, v_o, kk_o, b_o, g_o, bonus_o) = refs
    else:
        (x_ref, xp_ref, g_ref, mu_ref, wrkv_ref, w0_ref, wla_ref, wlb_ref, a0_ref, ala_ref, alb_ref,
         gla_ref, glb_ref, kk_ref, ka_ref, rk_ref, ind_ref, indt_ref,
         r_o, w_o, k_o, v_o, kk_o, b_o, g_o, bonus_o) = refs
    i = pl.program_id(0)
    gn = g_ref[...]
    h = _rms(x_ref[...], gn)
    hp = _rms(xp_ref[7:8, :], gn)
    hp = jnp.where(i % tiles_per_seq == 0, 0.0, hp)
    row = lax.broadcasted_iota(jnp.int32, h.shape, 0)
    hs = jnp.where(row == 0, hp, pltpu.roll(h, 1, 0))
    xx = hs - h

    def mix(p):
        return (h + xx * mu_ref[p:p + 1, :]).astype(BF16)

    xr, xk, xv, xw, xa, xg = (mix(p) for p in range(6))
    r = _dot(xr, wrkv_ref[0])
    k = _dot(xk, wrkv_ref[1])
    v = _dot(xv, wrkv_ref[2])
    lw = w0_ref[...] + _dot(jnp.tanh(_dot(xw, wla_ref[...])).astype(BF16), wlb_ref[...])
    log_w = -jax.nn.softplus(-lw) - 0.5
    w_o[...] = -jnp.exp(log_w)
    if has_vmix:
        mixv = _sigmoid(v0_ref[...] + _dot(_dot(xv, vla_ref[...]).astype(BF16), vlb_ref[...]))
        v = v + (vf_ref[...] - v) * mixv
    a = _sigmoid(a0_ref[...] + _dot(_dot(xa, ala_ref[...]).astype(BF16), alb_ref[...]))
    g_o[...] = _dot(_sigmoid(_dot(xg, gla_ref[...])).astype(BF16), glb_ref[...])
    kk = k * kk_ref[...]
    nrm = jnp.sqrt(_head_sum(kk * kk, ind_ref, indt_ref))
    kk = kk / jnp.maximum(nrm, 1e-12)
    k = k * (1.0 + (a - 1.0) * ka_ref[...])
    r_o[...] = r
    k_o[...] = k
    v_o[...] = v
    kk_o[...] = kk
    b_o[...] = kk * a
    bonus_o[...] = _head_sum(r * k * rk_ref[...], ind_ref, indt_ref) * v


def _rwkv_in(x, v_first, p, t_len):
    m = x.shape[0]
    tm = TM_RWKV
    has_vmix = v_first is not None
    prev_spec = pl.BlockSpec((8, D_MODEL), lambda i: (jnp.maximum(i * (tm // 8) - 1, 0), 0))
    ins = [x, x] + ([v_first] if has_vmix else [])
    specs = [_row_spec(tm), prev_spec] + ([_row_spec(tm)] if has_vmix else [])

    def add(a, spec=None):
        ins.append(a)
        specs.append(spec if spec is not None else _const_spec(a.shape))

    add(p["g"]); add(p["mu"]); add(p["w_rkv"]); add(p["w0"]); add(p["w_la"]); add(p["w_lb"])
    add(p["a0"]); add(p["a_la"]); add(p["a_lb"])
    if has_vmix:
        add(p["v0"]); add(p["v_la"]); add(p["v_lb"])
    add(p["g_la"]); add(p["g_lb"]); add(p["k_k"]); add(p["k_a"]); add(p["r_k"]); add(p["ind"]); add(p["indt"])
    out = jax.ShapeDtypeStruct((m, D_MODEL), F32)
    return pl.pallas_call(
        functools.partial(_rwkv_in_kernel, tiles_per_seq=t_len // tm, has_vmix=has_vmix),
        grid=(m // tm,),
        in_specs=specs,
        out_specs=[_row_spec(tm)] * 8,
        out_shape=[out] * 8,
        compiler_params=_params(),
        name="rwkv_in",
    )(*ins)


def _wkv_kernel(r_ref, w_ref, k_ref, v_ref, kk_ref, b_ref, y_ref, s_ref, *, nb, npairs):
    lc, hd, pw = WKV_CHUNK, HEAD_DIM, PAIR

    @pl.when(pl.program_id(0) == 0)
    def _():
        s_ref[...] = jnp.zeros_like(s_ref)

    lane = lax.broadcasted_iota(jnp.int32, (lc, pw), 1)
    row = lax.broadcasted_iota(jnp.int32, (lc, pw), 0)
    m0 = lane < hd
    col = jnp.where(m0, lane, lane - hd)
    strict = col < row
    incl = col <= row
    m0w = (lax.broadcasted_iota(jnp.int32, (lc, 2 * pw), 1) % pw) < hd
    r2 = lax.broadcasted_iota(jnp.int32, (pw, pw), 0)
    c2 = lax.broadcasted_iota(jnp.int32, (pw, pw), 1)
    bd = (r2 < hd) == (c2 < hd)
    tr = lax.broadcasted_iota(jnp.int32, (lc, 3 * lc), 0)
    tc = lax.broadcasted_iota(jnp.int32, (lc, 3 * lc), 1) % lc
    tri3 = jnp.where(tc <= tr, 1.0, 0.0).astype(BF16)

    def bdiag(x):
        z = jnp.zeros_like(x)
        return jnp.concatenate([jnp.where(m0, x, z), jnp.where(m0, z, x)], axis=0)

    chains = []
    for bi in range(nb):
        w = w_ref[bi]
        cum = _dot(tri3, jnp.concatenate(_split3(w), axis=0))
        tot = cum[lc - 1:lc, :]
        e_incl = jnp.exp(cum)
        e_excl = jnp.exp(cum - w)
        e_inv = jnp.exp(-cum)
        e_rem = jnp.exp(tot - cum)
        k = k_ref[bi]
        b = b_ref[bi]
        chains.extend((bi, p,
                       (-(kk_ref[bi] * e_excl)).astype(BF16), (r_ref[bi] * e_incl).astype(BF16),
                       (b * e_inv).astype(BF16), (k * e_inv).astype(BF16),
                       (b * e_rem).astype(BF16), (k * e_rem).astype(BF16),
                       v_ref[bi].astype(BF16), jnp.exp(tot)) for p in range(npairs))

    for g0 in range(0, len(chains), WKV_GROUP):
        grp = chains[g0:g0 + WKV_GROUP]
        n = len(grp)
        sls = [slice(c[1] * pw, (c[1] + 1) * pw) for c in grp]
        at, rt, bt, kt, bh, kh, vp, pt = ([c[j][:, sl] for c, sl in zip(grp, sls)] for j in range(2, 10))
        sidx = [c[0] * npairs + c[1] for c in grp]
        ar = [jnp.concatenate([at[i], rt[i]], axis=0) for i in range(n)]
        g = [_dot_nt(ar[i], jnp.concatenate([bdiag(bt[i]), bdiag(kt[i])], axis=0)) for i in range(n)]
        nmat = [jnp.where(strict, g[i][:lc, :pw], 0.0) for i in range(n)]
        aak = [jnp.where(strict, g[i][:lc, pw:], 0.0).astype(BF16) for i in range(n)]
        arbk = [jnp.concatenate([jnp.where(incl, g[i][lc:, :pw], 0.0), jnp.where(incl, g[i][lc:, pw:], 0.0)],
                                axis=1).astype(BF16) for i in range(n)]
        s = [s_ref[sidx[i]] for i in range(n)]
        vbd = [bdiag(vp[i]) for i in range(n)]
        ars = [_dot_nt(ar[i], s[i].astype(BF16)) for i in range(n)]
        wmat = [ars[i][:lc] + _dot(aak[i], vbd[i]) for i in range(n)]
        rs = [ars[i][lc:] for i in range(n)]
        for it in range(6):
            if it < 5:
                o = []
                for i in range(n):
                    pm = jnp.concatenate([nmat[i], wmat[i]], axis=1).astype(BF16)
                    z = jnp.zeros_like(pm)
                    rhs = jnp.concatenate([jnp.where(m0w, pm, z), jnp.where(m0w, z, pm)], axis=0)
                    o.append(_dot(nmat[i].astype(BF16), rhs))
                wmat = [wmat[i] + o[i][:, pw:] for i in range(n)]
                nmat = [o[i][:, :pw] for i in range(n)]
            else:
                o = [_dot(nmat[i].astype(BF16), bdiag(wmat[i].astype(BF16))) for i in range(n)]
                wmat = [wmat[i] + o[i] for i in range(n)]
        u16 = [wmat[i].astype(BF16) for i in range(n)]
        y = [rs[i] + _dot(arbk[i], jnp.concatenate([bdiag(u16[i]), vbd[i]], axis=0)) for i in range(n)]
        upd = [_dot_tn(jnp.concatenate([u16[i], vp[i]], axis=0), jnp.concatenate([bh[i], kh[i]], axis=0))
               for i in range(n)]
        for i in range(n):
            y_ref[grp[i][0], :, sls[i]] = y[i]
            s_ref[sidx[i]] = s[i] * pt[i] + jnp.where(bd, upd[i], 0.0)


def _wkv(r, w, k, v, kk, b):
    bsz, t_len, d = r.shape
    npairs = d // PAIR
    spec = pl.BlockSpec((bsz, WKV_CHUNK, d), lambda t: (0, t, 0))
    return pl.pallas_call(
        functools.partial(_wkv_kernel, nb=bsz, npairs=npairs),
        grid=(t_len // WKV_CHUNK,),
        in_specs=[spec] * 6,
        out_specs=spec,
        out_shape=jax.ShapeDtypeStruct((bsz, t_len, d), F32),
        scratch_shapes=[pltpu.VMEM((bsz * npairs, PAIR, PAIR), F32)],
        compiler_params=_params(),
        name="wkv7",
    )(r, w, k, v, kk, b)


def _rwkv_out_kernel(y_ref, bonus_ref, gate_ref, x_ref, lng_ref, lnb_ref, wo_ref, ind_ref, indt_ref,
                     g_ref, w1_ref, w2_ref, fg_ref, o_ref, *, final_norm):
    y = y_ref[...]
    mu = _head_sum(y, ind_ref, indt_ref) * (1.0 / HEAD_DIM)
    yc = y - mu
    var = _head_sum(yc * yc, ind_ref, indt_ref) * (1.0 / HEAD_DIM)
    yn = yc * lax.rsqrt(var + GN_EPS) * lng_ref[...] + lnb_ref[...]
    out = ((yn + bonus_ref[...]) * gate_ref[...]).astype(BF16)
    x = x_ref[...] + _dot(out, wo_ref[...])
    out = _ffn_body(x, g_ref[...], w1_ref, w2_ref)
    if final_norm:
        out = _rms(out, fg_ref[...])
    o_ref[...] = out


def _rwkv_out(y, bonus, gate, x, p, g, w1, w2, fg, final_norm):
    m = x.shape[0]
    tm = TM_RWKV
    vec = _const_spec((1, D_MODEL))
    return pl.pallas_call(
        functools.partial(_rwkv_out_kernel, final_norm=final_norm),
        grid=(m // tm,),
        in_specs=[_row_spec(tm)] * 4 + [vec, vec, _const_spec((D_MODEL, D_MODEL)),
                                        _const_spec(p["ind"].shape), _const_spec(p["indt"].shape), vec,
                                        _const_spec((D_MODEL, FFN_HIDDEN)), _const_spec((FFN_HIDDEN, D_MODEL)),
                                        vec],
        out_specs=_row_spec(tm),
        out_shape=jax.ShapeDtypeStruct((m, D_MODEL), F32),
        compiler_params=_params(),
        name="rwkv_out_ffn",
    )(y, bonus, gate, x, p["ln_g"], p["ln_b"], p["w_o"], p["ind"], p["indt"], g, w1, w2, fg)


def kernel(x, norm_mix_g, norm_ffn_g, final_norm_g, ffn_w1, ffn_w2, sgu_w_in, sgu_b_in, sgu_ln_g, sgu_ln_b, sgu_w_s, sgu_b_s, sgu_w_out, rwkv_mu, rwkv_w_rkv, rwkv_w0, rwkv_w_lora_a, rwkv_w_lora_b, rwkv_a0, rwkv_a_lora_a, rwkv_a_lora_b, rwkv_v0, rwkv_v_lora_a, rwkv_v_lora_b, rwkv_g_lora_a, rwkv_g_lora_b, rwkv_k_k, rwkv_k_a, rwkv_r_k, rwkv_ln_g, rwkv_ln_b, rwkv_w_o):
    bsz, t_len, d = x.shape
    m = bsz * t_len
    row = lambda a: a.reshape(1, -1)
    head_of_lane = jnp.arange(d) // HEAD_DIM
    ind = (head_of_lane[:, None] == jnp.arange(LANES)[None, :]).astype(BF16)
    indt = jnp.concatenate([ind.T, ind.T], axis=0)
    xf = x.reshape(m, d)
    v_first = None
    for i in range(DEPTH):
        j = i // 2
        w1 = ffn_w1[i].astype(BF16)
        w2 = ffn_w2[i].astype(BF16)
        if i % 2 == 0:
            xf = _sgu(xf, row(norm_mix_g[i]), sgu_w_in[j].astype(BF16), row(sgu_b_in[j]), row(sgu_ln_g[j]),
                      row(sgu_ln_b[j]), sgu_w_s[j], sgu_b_s[j][:, :, None], sgu_w_out[j].astype(BF16))
            xf = _ffn(xf, row(norm_ffn_g[i]), w1, w2)
        else:
            p = dict(g=row(norm_mix_g[i]), mu=rwkv_mu[j], w_rkv=rwkv_w_rkv[j].astype(BF16), w0=row(rwkv_w0[j]),
                     w_la=rwkv_w_lora_a[j].astype(BF16), w_lb=rwkv_w_lora_b[j].astype(BF16),
                     a0=row(rwkv_a0[j]), a_la=rwkv_a_lora_a[j].astype(BF16), a_lb=rwkv_a_lora_b[j].astype(BF16),
                     g_la=rwkv_g_lora_a[j].astype(BF16), g_lb=rwkv_g_lora_b[j].astype(BF16),
                     k_k=row(rwkv_k_k[j]), k_a=row(rwkv_k_a[j]), r_k=row(rwkv_r_k[j]),
                     ln_g=row(rwkv_ln_g[j]), ln_b=row(rwkv_ln_b[j]), w_o=rwkv_w_o[j].astype(BF16),
                     ind=ind, indt=indt)
            if j > 0:
                p.update(v0=row(rwkv_v0[j - 1]), v_la=rwkv_v_lora_a[j - 1].astype(BF16),
                         v_lb=rwkv_v_lora_b[j - 1].astype(BF16))
            r, w, k, v, kk, b, gate, bonus = _rwkv_in(xf, v_first if j > 0 else None, p, t_len)
            if j == 0:
                v_first = v
            sh = (bsz, t_len, d)
            y = _wkv(r.reshape(sh), w.reshape(sh), k.reshape(sh), v.reshape(sh), kk.reshape(sh), b.reshape(sh))
            xf = _rwkv_out(y.reshape(m, d), bonus, gate, xf, p, row(norm_ffn_g[i]), w1, w2, row(final_norm_g),
                           i == DEPTH - 1)
    return xf.reshape(bsz, t_len, d)
```

```python
import functools

import jax
import jax.numpy as jnp
from jax import lax
from jax.experimental import pallas as pl
from jax.experimental.pallas import tpu as pltpu

F32 = jnp.float32
BF16 = jnp.bfloat16

D_MODEL = 1024
DEPTH = 4
CHUNK = 64
SGU_BLOCK = 128
SGU_WIDTH = 2 * D_MODEL
SGU_GROUPS = 16
SGU_GROUP_DIM = SGU_WIDTH // SGU_GROUPS
HEAD_DIM = 64
HEADS = D_MODEL // HEAD_DIM
FFN_HIDDEN = 4 * D_MODEL
RMS_EPS = 1e-6
LN_EPS = 1e-5
GN_EPS = 64e-5

LANES = 128
WKV_CHUNK = 64
PAIR = 2 * HEAD_DIM
WKV_GROUP = 16

TM_FFN = 512
TM_SGU = 512
TM_RWKV = 256
TM_OUT = 512
VMEM_LIMIT = 56 * 1024 * 1024


def _dot(a, b):
    return jnp.dot(a, b, preferred_element_type=F32)


def _dot_nt(a, b):
    return lax.dot_general(a, b, (((1,), (1,)), ((), ())), preferred_element_type=F32)


def _dot_tn(a, b):
    return lax.dot_general(a, b, (((0,), (0,)), ((), ())), preferred_element_type=F32)


def _split3(x):
    hi = x.astype(BF16)
    r1 = x - hi.astype(F32)
    mid = r1.astype(BF16)
    lo = (r1 - mid.astype(F32)).astype(BF16)
    return hi, mid, lo


def _rms(x, g):
    return x * lax.rsqrt(jnp.mean(x * x, axis=-1, keepdims=True) + RMS_EPS) * g


def _gelu(x):
    return 0.5 * x * (1.0 + lax.erf(x * 0.7071067811865476))


def _sigmoid(x):
    return 1.0 / (1.0 + jnp.exp(-x))


def _const_spec(shape):
    nd = len(shape)
    return pl.BlockSpec(shape, lambda i: (0,) * nd, pipeline_mode=pl.Buffered(1))


def _row_spec(tm, width=D_MODEL):
    return pl.BlockSpec((tm, width), lambda i: (i, 0))


def _params():
    return pltpu.CompilerParams(dimension_semantics=("arbitrary",), vmem_limit_bytes=VMEM_LIMIT)


def _ffn_body(x, g, w1_ref, w2_ref):
    xn = _rms(x, g).astype(BF16)
    acc = x
    for c in range(FFN_HIDDEN // D_MODEL):
        sl = slice(c * D_MODEL, (c + 1) * D_MODEL)
        h = jnp.maximum(_dot(xn, w1_ref[:, sl]), 0.0)
        acc = acc + _dot((h * h).astype(BF16), w2_ref[sl, :])
    return acc


def _ffn_kernel(x_ref, g_ref, w1_ref, w2_ref, o_ref):
    o_ref[...] = _ffn_body(x_ref[...], g_ref[...], w1_ref, w2_ref)


def _ffn(x, g, w1, w2):
    m = x.shape[0]
    return pl.pallas_call(
        _ffn_kernel,
        grid=(m // TM_FFN,),
        in_specs=[_row_spec(TM_FFN), _const_spec((1, D_MODEL)), _const_spec((D_MODEL, FFN_HIDDEN)),
                  _const_spec((FFN_HIDDEN, D_MODEL))],
        out_specs=_row_spec(TM_FFN),
        out_shape=jax.ShapeDtypeStruct((m, D_MODEL), F32),
        compiler_params=_params(),
        name="ffn",
    )(x, g, w1, w2)


def _sgu_kernel(x_ref, g_ref, win_ref, bin_ref, lng_ref, lnb_ref, ws_ref, bs_ref, wout_ref, o_ref,
                vn_ref, y_ref):
    x = x_ref[...]
    xn = _rms(x, g_ref[...]).astype(BF16)
    e = SGU_WIDTH
    zv = _gelu(_dot(xn, win_ref[:, e:]) + bin_ref[:, e:])
    mu = jnp.mean(zv, axis=-1, keepdims=True)
    zc = zv - mu
    var = jnp.mean(zc * zc, axis=-1, keepdims=True)
    vn_ref[...] = (zc * lax.rsqrt(var + LN_EPS) * lng_ref[...] + lnb_ref[...]).astype(BF16)
    nblk = TM_SGU // SGU_BLOCK
    qi = lax.broadcasted_iota(jnp.int32, (SGU_BLOCK, SGU_BLOCK), 0) // CHUNK
    kj = lax.broadcasted_iota(jnp.int32, (SGU_BLOCK, SGU_BLOCK), 1) // CHUNK
    causal = kj <= qi
    gw = SGU_GROUP_DIM
    for gp in range(SGU_GROUPS // 2):
        cs2 = slice(2 * gp * gw, (2 * gp + 2) * gw)
        u2 = _gelu(_dot(xn, win_ref[:, cs2]) + bin_ref[:, cs2])
        for half in range(2):
            gi = 2 * gp + half
            cs = slice(gi * gw, (gi + 1) * gw)
            ws = jnp.where(causal, ws_ref[gi], 0.0).astype(BF16)
            rhs = jnp.concatenate([vn_ref[n * SGU_BLOCK:(n + 1) * SGU_BLOCK, cs] for n in range(nblk)], axis=1)
            s = _dot(ws, rhs) + bs_ref[gi]
            for n in range(nblk):
                rs = slice(n * SGU_BLOCK, (n + 1) * SGU_BLOCK)
                y_ref[rs, cs] = (u2[rs, half * gw:(half + 1) * gw] * s[:, n * gw:(n + 1) * gw]).astype(BF16)
    o_ref[...] = x + _dot(y_ref[...], wout_ref[...])


def _sgu(x, g, w_in, b_in, ln_g, ln_b, w_s, b_s, w_out):
    m = x.shape[0]
    e = SGU_WIDTH
    return pl.pallas_call(
        _sgu_kernel,
        grid=(m // TM_SGU,),
        in_specs=[_row_spec(TM_SGU), _const_spec((1, D_MODEL)), _const_spec((D_MODEL, 2 * e)),
                  _const_spec((1, 2 * e)), _const_spec((1, e)), _const_spec((1, e)),
                  _const_spec((SGU_GROUPS, SGU_BLOCK, SGU_BLOCK)), _const_spec((SGU_GROUPS, SGU_BLOCK, 1)),
                  _const_spec((e, D_MODEL))],
        out_specs=_row_spec(TM_SGU),
        out_shape=jax.ShapeDtypeStruct((m, D_MODEL), F32),
        scratch_shapes=[pltpu.VMEM((TM_SGU, e), BF16), pltpu.VMEM((TM_SGU, e), BF16)],
        compiler_params=_params(),
        name="sgu",
    )(x, g, w_in, b_in, ln_g, ln_b, w_s, b_s, w_out)


def _head_sum(x, ind_ref, indt_ref):
    s = _dot(x.astype(BF16), ind_ref[...])
    hi = s.astype(BF16)
    lo = (s - hi.astype(F32)).astype(BF16)
    return _dot(jnp.concatenate([hi, lo], axis=1), indt_ref[...])


def _rwkv_in_kernel(*refs, tiles_per_seq, has_vmix):
    if has_vmix:
        (x_ref, xp_ref, vf_ref, g_ref, mu_ref, wrkv_ref, w0_ref, wla_ref, wlb_ref, a0_ref, ala_ref, alb_ref,
         v0_ref, vla_ref, vlb_ref, gla_ref, glb_ref, kk_ref, ka_ref, rk_ref, ind_ref, indt_ref,
         r_o, w_o, k_o, v_o, kk_o, b_o, g_o, bonus_o) = refs
    else:
        (x_ref, xp_ref, g_ref, mu_ref, wrkv_ref, w0_ref, wla_ref, wlb_ref, a0_ref, ala_ref, alb_ref,
         gla_ref, glb_ref, kk_ref, ka_ref, rk_ref, ind_ref, indt_ref,
         r_o, w_o, k_o, v_o, kk_o, b_o, g_o, bonus_o) = refs
    i = pl.program_id(0)
    gn = g_ref[...]
    h = _rms(x_ref[...], gn)
    hp = _rms(xp_ref[7:8, :], gn)
    hp = jnp.where(i % tiles_per_seq == 0, 0.0, hp)
    row = lax.broadcasted_iota(jnp.int32, h.shape, 0)
    hs = jnp.where(row == 0, hp, pltpu.roll(h, 1, 0))
    xx = hs - h

    def mix(p):
        return (h + xx * mu_ref[p:p + 1, :]).astype(BF16)

    xr, xk, xv, xw, xa, xg = (mix(p) for p in range(6))
    r = _dot(xr, wrkv_ref[0])
    k = _dot(xk, wrkv_ref[1])
    v = _dot(xv, wrkv_ref[2])
    lw = w0_ref[...] + _dot(jnp.tanh(_dot(xw, wla_ref[...])).astype(BF16), wlb_ref[...])
    w_o[...] = -0.6065306597126334 * _sigmoid(lw)
    if has_vmix:
        mixv = _sigmoid(v0_ref[...] + _dot(_dot(xv, vla_ref[...]).astype(BF16), vlb_ref[...]))
        v = v + (vf_ref[...] - v) * mixv
    a = _sigmoid(a0_ref[...] + _dot(_dot(xa, ala_ref[...]).astype(BF16), alb_ref[...]))
    g_o[...] = _dot(_sigmoid(_dot(xg, gla_ref[...])).astype(BF16), glb_ref[...])
    kk = k * kk_ref[...]
    nrm = jnp.sqrt(_head_sum(kk * kk, ind_ref, indt_ref))
    kk = kk / jnp.maximum(nrm, 1e-12)
    k = k * (1.0 + (a - 1.0) * ka_ref[...])
    r_o[...] = r
    k_o[...] = k
    v_o[...] = v
    kk_o[...] = kk
    b_o[...] = kk * a
    bonus_o[...] = _head_sum(r * k * rk_ref[...], ind_ref, indt_ref) * v


def _rwkv_in(x, v_first, p, t_len):
    m = x.shape[0]
    tm = TM_RWKV
    has_vmix = v_first is not None
    prev_spec = pl.BlockSpec((8, D_MODEL), lambda i: (jnp.maximum(i * (tm // 8) - 1, 0), 0))
    ins = [x, x] + ([v_first] if has_vmix else [])
    specs = [_row_spec(tm), prev_spec] + ([_row_spec(tm)] if has_vmix else [])

    def add(a, spec=None):
        ins.append(a)
        specs.append(spec if spec is not None else _const_spec(a.shape))

    add(p["g"]); add(p["mu"]); add(p["w_rkv"]); add(p["w0"]); add(p["w_la"]); add(p["w_lb"])
    add(p["a0"]); add(p["a_la"]); add(p["a_lb"])
    if has_vmix:
        add(p["v0"]); add(p["v_la"]); add(p["v_lb"])
    add(p["g_la"]); add(p["g_lb"]); add(p["k_k"]); add(p["k_a"]); add(p["r_k"]); add(p["ind"]); add(p["indt"])
    out = jax.ShapeDtypeStruct((m, D_MODEL), F32)
    return pl.pallas_call(
        functools.partial(_rwkv_in_kernel, tiles_per_seq=t_len // tm, has_vmix=has_vmix),
        grid=(m // tm,),
        in_specs=specs,
        out_specs=[_row_spec(tm)] * 8,
        out_shape=[out] * 8,
        compiler_params=_params(),
        name="rwkv_in",
    )(*ins)


def _wkv_kernel(r_ref, w_ref, k_ref, v_ref, kk_ref, b_ref, y_ref, s_ref, *, nb, npairs):
    lc, hd, pw = WKV_CHUNK, HEAD_DIM, PAIR

    @pl.when(pl.program_id(0) == 0)
    def _():
        s_ref[...] = jnp.zeros_like(s_ref)

    lane = lax.broadcasted_iota(jnp.int32, (lc, pw), 1)
    row = lax.broadcasted_iota(jnp.int32, (lc, pw), 0)
    m0 = lane < hd
    col = jnp.where(m0, lane, lane - hd)
    strict = col < row
    incl = col <= row
    m0w = (lax.broadcasted_iota(jnp.int32, (lc, 2 * pw), 1) % pw) < hd
    r2 = lax.broadcasted_iota(jnp.int32, (pw, pw), 0)
    c2 = lax.broadcasted_iota(jnp.int32, (pw, pw), 1)
    bd = (r2 < hd) == (c2 < hd)
    tr = lax.broadcasted_iota(jnp.int32, (lc, 3 * lc), 0)
    tc = lax.broadcasted_iota(jnp.int32, (lc, 3 * lc), 1) % lc
    tri3 = jnp.where(tc <= tr, 1.0, 0.0).astype(BF16)

    def bdiag(x):
        z = jnp.zeros_like(x)
        return jnp.concatenate([jnp.where(m0, x, z), jnp.where(m0, z, x)], axis=0)

    chains = []
    for bi in range(nb):
        w = w_ref[bi]
        cum = _dot(tri3, jnp.concatenate(_split3(w), axis=0))
        tot = cum[lc - 1:lc, :]
        e_incl = jnp.exp(cum)
        e_excl = jnp.exp(cum - w)
        e_inv = jnp.exp(-cum)
        e_rem = jnp.exp(tot - cum)
        k = k_ref[bi]
        b = b_ref[bi]
        chains.extend((bi, p,
                       (-(kk_ref[bi] * e_excl)).astype(BF16), (r_ref[bi] * e_incl).astype(BF16),
                       (b * e_inv).astype(BF16), (k * e_inv).astype(BF16),
                       (b * e_rem).astype(BF16), (k * e_rem).astype(BF16),
                       v_ref[bi].astype(BF16), jnp.exp(tot)) for p in range(npairs))

    for g0 in range(0, len(chains), WKV_GROUP):
        grp = chains[g0:g0 + WKV_GROUP]
        n = len(grp)
        sls = [slice(c[1] * pw, (c[1] + 1) * pw) for c in grp]
        at, rt, bt, kt, bh, kh, vp, pt = ([c[j][:, sl] for c, sl in zip(grp, sls)] for j in range(2, 10))
        sidx = [c[0] * npairs + c[1] for c in grp]
        ar = [jnp.concatenate([at[i], rt[i]], axis=0) for i in range(n)]
        g = [_dot_nt(ar[i], jnp.concatenate([bdiag(bt[i]), bdiag(kt[i])], axis=0)) for i in range(n)]
        nmat = [jnp.where(strict, g[i][:lc, :pw], 0.0) for i in range(n)]
        aak = [jnp.where(strict, g[i][:lc, pw:], 0.0).astype(BF16) for i in range(n)]
        arbk = [jnp.concatenate([jnp.where(incl, g[i][lc:, :pw], 0.0), jnp.where(incl, g[i][lc:, pw:], 0.0)],
                                axis=1).astype(BF16) for i in range(n)]
        s = [s_ref[sidx[i]] for i in range(n)]
        vbd = [bdiag(vp[i]) for i in range(n)]
        ars = [_dot_nt(ar[i], s[i].astype(BF16)) for i in range(n)]
        wmat = [ars[i][:lc] + _dot(aak[i], vbd[i]) for i in range(n)]
        rs = [ars[i][lc:] for i in range(n)]
        for it in range(6):
            if it < 5:
                o = []
                for i in range(n):
                    pm = jnp.concatenate([nmat[i], wmat[i]], axis=1).astype(BF16)
                    z = jnp.zeros_like(pm)
                    rhs = jnp.concatenate([jnp.where(m0w, pm, z), jnp.where(m0w, z, pm)], axis=0)
                    o.append(_dot(nmat[i].astype(BF16), rhs))
                wmat = [wmat[i] + o[i][:, pw:] for i in range(n)]
                nmat = [o[i][:, :pw] for i in range(n)]
            else:
                o = [_dot(nmat[i].astype(BF16), bdiag(wmat[i].astype(BF16))) for i in range(n)]
                wmat = [wmat[i] + o[i] for i in range(n)]
        u16 = [wmat[i].astype(BF16) for i in range(n)]
        y = [rs[i] + _dot(arbk[i], jnp.concatenate([bdiag(u16[i]), vbd[i]], axis=0)) for i in range(n)]
        upd = [_dot_tn(jnp.concatenate([u16[i], vp[i]], axis=0), jnp.concatenate([bh[i], kh[i]], axis=0))
               for i in range(n)]
        for i in range(n):
            y_ref[grp[i][0], :, sls[i]] = y[i]
            s_ref[sidx[i]] = s[i] * pt[i] + jnp.where(bd, upd[i], 0.0)


def _wkv(r, w, k, v, kk, b):
    bsz, t_len, d = r.shape
    npairs = d // PAIR
    spec = pl.BlockSpec((bsz, WKV_CHUNK, d), lambda t: (0, t, 0))
    return pl.pallas_call(
        functools.partial(_wkv_kernel, nb=bsz, npairs=npairs),
        grid=(t_len // WKV_CHUNK,),
        in_specs=[spec] * 6,
        out_specs=spec,
        out_shape=jax.ShapeDtypeStruct((bsz, t_len, d), F32),
        scratch_shapes=[pltpu.VMEM((bsz * npairs, PAIR, PAIR), F32)],
        compiler_params=_params(),
        name="wkv7",
    )(r, w, k, v, kk, b)


def _rwkv_out_kernel(y_ref, bonus_ref, gate_ref, x_ref, lng_ref, lnb_ref, wo_ref, ind_ref, indt_ref,
                     g_ref, w1_ref, w2_ref, fg_ref, o_ref, *, final_norm):
    y = y_ref[...]
    mu = _head_sum(y, ind_ref, indt_ref) * (1.0 / HEAD_DIM)
    yc = y - mu
    var = _head_sum(yc * yc, ind_ref, indt_ref) * (1.0 / HEAD_DIM)
    yn = yc * lax.rsqrt(var + GN_EPS) * lng_ref[...] + lnb_ref[...]
    out = ((yn + bonus_ref[...]) * gate_ref[...]).astype(BF16)
    x = x_ref[...] + _dot(out, wo_ref[...])
    out = _ffn_body(x, g_ref[...], w1_ref, w2_ref)
    if final_norm:
        out = _rms(out, fg_ref[...])
    o_ref[...] = out


def _rwkv_out(y, bonus, gate, x, p, g, w1, w2, fg, final_norm):
    m = x.shape[0]
    tm = TM_OUT
    vec = _const_spec((1, D_MODEL))
    return pl.pallas_call(
        functools.partial(_rwkv_out_kernel, final_norm=final_norm),
        grid=(m // tm,),
        in_specs=[_row_spec(tm)] * 4 + [vec, vec, _const_spec((D_MODEL, D_MODEL)),
                                        _const_spec(p["ind"].shape), _const_spec(p["indt"].shape), vec,
                                        _const_spec((D_MODEL, FFN_HIDDEN)), _const_spec((FFN_HIDDEN, D_MODEL)),
                                        vec],
        out_specs=_row_spec(tm),
        out_shape=jax.ShapeDtypeStruct((m, D_MODEL), F32),
        compiler_params=_params(),
        name="rwkv_out_ffn",
    )(y, bonus, gate, x, p["ln_g"], p["ln_b"], p["w_o"], p["ind"], p["indt"], g, w1, w2, fg)


def kernel(x, norm_mix_g, norm_ffn_g, final_norm_g, ffn_w1, ffn_w2, sgu_w_in, sgu_b_in, sgu_ln_g, sgu_ln_b, sgu_w_s, sgu_b_s, sgu_w_out, rwkv_mu, rwkv_w_rkv, rwkv_w0, rwkv_w_lora_a, rwkv_w_lora_b, rwkv_a0, rwkv_a_lora_a, rwkv_a_lora_b, rwkv_v0, rwkv_v_lora_a, rwkv_v_lora_b, rwkv_g_lora_a, rwkv_g_lora_b, rwkv_k_k, rwkv_k_a, rwkv_r_k, rwkv_ln_g, rwkv_ln_b, rwkv_w_o):
    bsz, t_len, d = x.shape
    m = bsz * t_len
    row = lambda a: a.reshape(1, -1)
    head_of_lane = jnp.arange(d) // HEAD_DIM
    ind = (head_of_lane[:, None] == jnp.arange(LANES)[None, :]).astype(BF16)
    indt = jnp.concatenate([ind.T, ind.T], axis=0)
    xf = x.reshape(m, d)
    v_first = None
    for i in range(DEPTH):
        j = i // 2
        w1 = ffn_w1[i].astype(BF16)
        w2 = ffn_w2[i].astype(BF16)
        if i % 2 == 0:
            xf = _sgu(xf, row(norm_mix_g[i]), sgu_w_in[j].astype(BF16), row(sgu_b_in[j]), row(sgu_ln_g[j]),
                      row(sgu_ln_b[j]), sgu_w_s[j], sgu_b_s[j][:, :, None], sgu_w_out[j].astype(BF16))
            xf = _ffn(xf, row(norm_ffn_g[i]), w1, w2)
        else:
            p = dict(g=row(norm_mix_g[i]), mu=rwkv_mu[j], w_rkv=rwkv_w_rkv[j].astype(BF16), w0=row(rwkv_w0[j]),
                     w_la=rwkv_w_lora_a[j].astype(BF16), w_lb=rwkv_w_lora_b[j].astype(BF16),
                     a0=row(rwkv_a0[j]), a_la=rwkv_a_lora_a[j].astype(BF16), a_lb=rwkv_a_lora_b[j].astype(BF16),
                     g_la=rwkv_g_lora_a[j].astype(BF16), g_lb=rwkv_g_lora_b[j].astype(BF16),
                     k_k=row(rwkv_k_k[j]), k_a=row(rwkv_k_a[j]), r_k=row(rwkv_r_k[j]),
                     ln_g=row(rwkv_ln_g[j]), ln_b=row(rwkv_ln_b[j]), w_o=rwkv_w_o[j].astype(BF16),
                     ind=ind, indt=indt)
            if j > 0:
                p.update(v0=row(rwkv_v0[j - 1]), v_la=rwkv_v_lora_a[j - 1].astype(BF16),
                         v_lb=rwkv_v_lora_b[j - 1].astype(BF16))
            r, w, k, v, kk, b, gate, bonus = _rwkv_in(xf, v_first if j > 0 else None, p, t_len)
            if j == 0:
                v_first = v
            sh = (bsz, t_len, d)
            y = _wkv(r.reshape(sh), w.reshape(sh), k.reshape(sh), v.reshape(sh), kk.reshape(sh), b.reshape(sh))
            xf = _rwkv_out(y.reshape(m, d), bonus, gate, xf, p, row(norm_ffn_g[i]), w1, w2, row(final_norm_g),
                           i == DEPTH - 1)
    return xf.reshape(bsz, t_len, d)
```

```python
import functools

import jax
import jax.numpy as jnp
from jax import lax
from jax.experimental import pallas as pl
from jax.experimental.pallas import tpu as pltpu

F32 = jnp.float32
BF16 = jnp.bfloat16

D_MODEL = 1024
DEPTH = 4
CHUNK = 64
SGU_BLOCK = 128
SGU_WIDTH = 2 * D_MODEL
SGU_GROUPS = 16
SGU_GROUP_DIM = SGU_WIDTH // SGU_GROUPS
HEAD_DIM = 64
HEADS = D_MODEL // HEAD_DIM
FFN_HIDDEN = 4 * D_MODEL
RMS_EPS = 1e-6
LN_EPS = 1e-5
GN_EPS = 64e-5

LANES = 128
WKV_CHUNK = 64
PAIR = 2 * HEAD_DIM
WKV_GROUP = 16

TM_FFN = 512
TM_SGU = 512
TM_RWKV = 256
TM_OUT = 512
VMEM_LIMIT = 56 * 1024 * 1024


def _dot(a, b):
    return jnp.dot(a, b, preferred_element_type=F32)


def _dot_nt(a, b):
    return lax.dot_general(a, b, (((1,), (1,)), ((), ())), preferred_element_type=F32)


def _split3(x):
    hi = x.astype(BF16)
    r1 = x - hi.astype(F32)
    mid = r1.astype(BF16)
    lo = (r1 - mid.astype(F32)).astype(BF16)
    return hi, mid, lo


def _rms(x, g):
    return x * lax.rsqrt(jnp.mean(x * x, axis=-1, keepdims=True) + RMS_EPS) * g


def _gelu(x):
    return 0.5 * x * (1.0 + lax.erf(x * 0.7071067811865476))


def _sigmoid(x):
    return 1.0 / (1.0 + jnp.exp(-x))


def _const_spec(shape):
    nd = len(shape)
    return pl.BlockSpec(shape, lambda i: (0,) * nd, pipeline_mode=pl.Buffered(1))


def _layer_spec(stacked, layer):
    shape = stacked.shape[1:]
    return pl.BlockSpec((None,) + shape, lambda i: (layer,) + (0,) * len(shape), pipeline_mode=pl.Buffered(1))


def _row_spec(tm, width=D_MODEL):
    return pl.BlockSpec((tm, width), lambda i: (i, 0))


def _params():
    return pltpu.CompilerParams(dimension_semantics=("arbitrary",), vmem_limit_bytes=VMEM_LIMIT)


def _ffn_body(x, g, w1_ref, w2_ref):
    xn = _rms(x, g).astype(BF16)
    acc = x
    for c in range(FFN_HIDDEN // D_MODEL):
        sl = slice(c * D_MODEL, (c + 1) * D_MODEL)
        h = jnp.maximum(_dot(xn, w1_ref[:, sl]), 0.0)
        acc = acc + _dot((h * h).astype(BF16), w2_ref[sl, :])
    return acc


def _ffn_kernel(x_ref, g_ref, w1_ref, w2_ref, o_ref):
    o_ref[...] = _ffn_body(x_ref[...], g_ref[...], w1_ref, w2_ref)


def _ffn(x, g, w1, w2, layer):
    m = x.shape[0]
    return pl.pallas_call(
        _ffn_kernel,
        grid=(m // TM_FFN,),
        in_specs=[_row_spec(TM_FFN), _const_spec((1, D_MODEL)), _layer_spec(w1, layer), _layer_spec(w2, layer)],
        out_specs=_row_spec(TM_FFN),
        out_shape=jax.ShapeDtypeStruct((m, D_MODEL), F32),
        compiler_params=_params(),
        name="ffn",
    )(x, g, w1, w2)


def _sgu_kernel(x_ref, g_ref, win_ref, bin_ref, lng_ref, lnb_ref, ws_ref, bs_ref, wout_ref, o_ref,
                vn_ref, y_ref, u_ref):
    x = x_ref[...]
    xn = _rms(x, g_ref[...]).astype(BF16)
    e = SGU_WIDTH
    zv = _gelu(_dot(xn, win_ref[:, e:]) + bin_ref[:, e:])
    u_ref[...] = _gelu(_dot(xn, win_ref[:, :e]) + bin_ref[:, :e])
    mu = jnp.mean(zv, axis=-1, keepdims=True)
    zc = zv - mu
    var = jnp.mean(zc * zc, axis=-1, keepdims=True)
    vn_ref[...] = (zc * lax.rsqrt(var + LN_EPS) * lng_ref[...] + lnb_ref[...]).astype(BF16)
    nblk = TM_SGU // SGU_BLOCK
    qi = lax.broadcasted_iota(jnp.int32, (SGU_BLOCK, SGU_BLOCK), 0) // CHUNK
    kj = lax.broadcasted_iota(jnp.int32, (SGU_BLOCK, SGU_BLOCK), 1) // CHUNK
    causal = kj <= qi
    gw = SGU_GROUP_DIM
    for gi in range(SGU_GROUPS):
        cs = slice(gi * gw, (gi + 1) * gw)
        ws = jnp.where(causal, ws_ref[gi], 0.0).astype(BF16)
        rhs = jnp.concatenate([vn_ref[n * SGU_BLOCK:(n + 1) * SGU_BLOCK, cs] for n in range(nblk)], axis=1)
        s = _dot(ws, rhs) + bs_ref[gi]
        for n in range(nblk):
            rs = slice(n * SGU_BLOCK, (n + 1) * SGU_BLOCK)
            y_ref[rs, cs] = (u_ref[rs, cs] * s[:, n * gw:(n + 1) * gw]).astype(BF16)
    o_ref[...] = x + _dot(y_ref[...], wout_ref[...])


def _sgu(x, g, w_in, b_in, ln_g, ln_b, w_s, b_s, w_out, layer):
    m = x.shape[0]
    e = SGU_WIDTH
    return pl.pallas_call(
        _sgu_kernel,
        grid=(m // TM_SGU,),
        in_specs=[_row_spec(TM_SGU), _const_spec((1, D_MODEL)), _layer_spec(w_in, layer),
                  _const_spec((1, 2 * e)), _const_spec((1, e)), _const_spec((1, e)),
                  _const_spec((SGU_GROUPS, SGU_BLOCK, SGU_BLOCK)), _const_spec((SGU_GROUPS, SGU_BLOCK, 1)),
                  _layer_spec(w_out, layer)],
        out_specs=_row_spec(TM_SGU),
        out_shape=jax.ShapeDtypeStruct((m, D_MODEL), F32),
        scratch_shapes=[pltpu.VMEM((TM_SGU, e), BF16), pltpu.VMEM((TM_SGU, e), BF16), pltpu.VMEM((TM_SGU, e), F32)],
        compiler_params=_params(),
        name="sgu",
    )(x, g, w_in, b_in, ln_g, ln_b, w_s, b_s, w_out)


def _head_sum1(x, ind_ref):
    return _dot(x.astype(BF16), ind_ref[...])


def _head_sum2(s, indt_ref):
    hi = s.astype(BF16)
    lo = (s - hi.astype(F32)).astype(BF16)
    return _dot(jnp.concatenate([hi, lo], axis=1), indt_ref[...])


def _rwkv_in_kernel(*refs, tiles_per_seq, has_vmix):
    if has_vmix:
        (x_ref, xp_ref, vf_ref, g_ref, mu_ref, wrkv_ref, w0_ref, wla_ref, wlb_ref, a0_ref, ala_ref, alb_ref,
         v0_ref, vla_ref, vlb_ref, gla_ref, glb_ref, kk_ref, ka_ref, rk_ref, ind_ref, indt_ref,
         r_o, w_o, k_o, v_o, kk_o, b_o, g_o, bonus_o) = refs
    else:
        (x_ref, xp_ref, g_ref, mu_ref, wrkv_ref, w0_ref, wla_ref, wlb_ref, a0_ref, ala_ref, alb_ref,
         gla_ref, glb_ref, kk_ref, ka_ref, rk_ref, ind_ref, indt_ref,
         r_o, w_o, k_o, v_o, kk_o, b_o, g_o, bonus_o) = refs
    i = pl.program_id(0)
    gn = g_ref[...]
    h = _rms(x_ref[...], gn)
    hp = _rms(xp_ref[7:8, :], gn)
    hp = jnp.where(i % tiles_per_seq == 0, 0.0, hp)
    row = lax.broadcasted_iota(jnp.int32, h.shape, 0)
    hs = jnp.where(row == 0, hp, pltpu.roll(h, 1, 0))
    xx = hs - h

    def mix(p):
        return (h + xx * mu_ref[p:p + 1, :]).astype(BF16)

    xv = mix(2)
    tw = _dot(mix(3), wla_ref[...])
    ta = _dot(mix(4), ala_ref[...])
    tg = _dot(mix(5), gla_ref[...])
    if has_vmix:
        tv = _dot(xv, vla_ref[...])
    k = _dot(mix(1), wrkv_ref[1])
    r = _dot(mix(0), wrkv_ref[0])
    kk = k * kk_ref[...]
    ss = _head_sum1(kk * kk, ind_ref)
    lw = w0_ref[...] + _dot(jnp.tanh(tw).astype(BF16), wlb_ref[...])
    a = _sigmoid(a0_ref[...] + _dot(ta.astype(BF16), alb_ref[...]))
    nrm = jnp.sqrt(_head_sum2(ss, indt_ref))
    v = _dot(xv, wrkv_ref[2])
    r_o[...] = r
    k = k * (1.0 + (a - 1.0) * ka_ref[...])
    rk = _head_sum1(r * k * rk_ref[...], ind_ref)
    g_o[...] = _dot(_sigmoid(tg).astype(BF16), glb_ref[...])
    if has_vmix:
        mixv = _sigmoid(v0_ref[...] + _dot(tv.astype(BF16), vlb_ref[...]))
        v = v + (vf_ref[...] - v) * mixv
    rk = _head_sum2(rk, indt_ref)
    w_o[...] = -0.6065306597126334 * _sigmoid(lw)
    k_o[...] = k
    v_o[...] = v
    kk = kk / jnp.maximum(nrm, 1e-12)
    kk_o[...] = kk
    b_o[...] = kk * a
    bonus_o[...] = rk * v


def _rwkv_in(x, v_first, p, t_len):
    m = x.shape[0]
    tm = TM_RWKV
    has_vmix = v_first is not None
    prev_spec = pl.BlockSpec((8, D_MODEL), lambda i: (jnp.maximum(i * (tm // 8) - 1, 0), 0))
    ins = [x, x] + ([v_first] if has_vmix else [])
    specs = [_row_spec(tm), prev_spec] + ([_row_spec(tm)] if has_vmix else [])

    def add(a, spec=None):
        ins.append(a)
        specs.append(spec if spec is not None else _const_spec(a.shape))

    add(p["g"]); add(p["mu"]); add(p["w_rkv"], _layer_spec(p["w_rkv"], p["layer"]))
    add(p["w0"]); add(p["w_la"]); add(p["w_lb"])
    add(p["a0"]); add(p["a_la"]); add(p["a_lb"])
    if has_vmix:
        add(p["v0"]); add(p["v_la"]); add(p["v_lb"])
    add(p["g_la"]); add(p["g_lb"]); add(p["k_k"]); add(p["k_a"]); add(p["r_k"]); add(p["ind"]); add(p["indt"])
    out = jax.ShapeDtypeStruct((m, D_MODEL), F32)
    return pl.pallas_call(
        functools.partial(_rwkv_in_kernel, tiles_per_seq=t_len // tm, has_vmix=has_vmix),
        grid=(m // tm,),
        in_specs=specs,
        out_specs=[_row_spec(tm)] * 8,
        out_shape=[out] * 8,
        compiler_params=_params(),
        name="rwkv_in",
    )(*ins)


def _wkv_kernel(r_ref, w_ref, k_ref, v_ref, kk_ref, b_ref, y_ref, s_ref, *, nb, npairs):
    lc, hd, pw = WKV_CHUNK, HEAD_DIM, PAIR

    @pl.when(pl.program_id(0) == 0)
    def _():
        s_ref[...] = jnp.zeros_like(s_ref)

    lane = lax.broadcasted_iota(jnp.int32, (lc, pw), 1)
    row = lax.broadcasted_iota(jnp.int32, (lc, pw), 0)
    m0 = lane < hd
    col = jnp.where(m0, lane, lane - hd)
    strict = col < row
    incl = col <= row
    m0w = (lax.broadcasted_iota(jnp.int32, (lc, 2 * pw), 1) % pw) < hd
    tr = lax.broadcasted_iota(jnp.int32, (lc, 3 * lc), 0)
    tc = lax.broadcasted_iota(jnp.int32, (lc, 3 * lc), 1) % lc
    tri3 = jnp.where(tc <= tr, 1.0, 0.0).astype(BF16)

    def bdiag(x):
        z = jnp.zeros_like(x)
        return jnp.concatenate([jnp.where(m0, x, z), jnp.where(m0, z, x)], axis=0)

    chains = []
    for bi in range(nb):
        w = w_ref[bi]
        cum = _dot(tri3, jnp.concatenate(_split3(w), axis=0))
        tot = cum[lc - 1:lc, :]
        e_incl = jnp.exp(cum)
        e_excl = jnp.exp(cum - w)
        e_inv = jnp.exp(-cum)
        e_rem = jnp.exp(tot - cum)
        k = k_ref[bi]
        b = b_ref[bi]
        chains.extend((bi, p,
                       (-(kk_ref[bi] * e_excl)).astype(BF16), (r_ref[bi] * e_incl).astype(BF16),
                       (b * e_inv).astype(BF16), (k * e_inv).astype(BF16),
                       b * e_rem, k * e_rem,
                       v_ref[bi].astype(BF16), jnp.exp(tot)) for p in range(npairs))

    for g0 in range(0, len(chains), WKV_GROUP):
        grp = chains[g0:g0 + WKV_GROUP]
        n = len(grp)
        sls = [slice(c[1] * pw, (c[1] + 1) * pw) for c in grp]
        at, rt, bt, kt, bh, kh, vp, pt = ([c[j][:, sl] for c, sl in zip(grp, sls)] for j in range(2, 10))
        sidx = [c[0] * npairs + c[1] for c in grp]
        ar = [jnp.concatenate([at[i], rt[i]], axis=0) for i in range(n)]
        g = [_dot_nt(ar[i], jnp.concatenate([bdiag(bt[i]), bdiag(kt[i])], axis=0)) for i in range(n)]
        nmat = [jnp.where(strict, g[i][:lc, :pw], 0.0) for i in range(n)]
        aak = [jnp.where(strict, g[i][:lc, pw:], 0.0).astype(BF16) for i in range(n)]
        arbk = [jnp.concatenate([jnp.where(incl, g[i][lc:, :pw], 0.0), jnp.where(incl, g[i][lc:, pw:], 0.0)],
                                axis=1).astype(BF16) for i in range(n)]
        s = [s_ref[sidx[i]] for i in range(n)]
        vbd = [bdiag(vp[i]) for i in range(n)]
        sbd = [bdiag(s[i].astype(BF16)) for i in range(n)]
        wmat = [_dot(jnp.concatenate([at[i], aak[i]], axis=1), jnp.concatenate([sbd[i], vbd[i]], axis=0))
                for i in range(n)]
        rs = [_dot(rt[i], sbd[i]) for i in range(n)]
        for it in range(6):
            if it < 5:
                o = []
                for i in range(n):
                    pm = jnp.concatenate([nmat[i], wmat[i]], axis=1).astype(BF16)
                    z = jnp.zeros_like(pm)
                    rhs = jnp.concatenate([jnp.where(m0w, pm, z), jnp.where(m0w, z, pm)], axis=0)
                    o.append(_dot(nmat[i].astype(BF16), rhs))
                wmat = [wmat[i] + o[i][:, pw:] for i in range(n)]
                nmat = [o[i][:, :pw] for i in range(n)]
            else:
                o = [_dot(nmat[i].astype(BF16), bdiag(wmat[i].astype(BF16))) for i in range(n)]
                wmat = [wmat[i] + o[i] for i in range(n)]
        u16 = [wmat[i].astype(BF16) for i in range(n)]
        y = [rs[i] + _dot(arbk[i], jnp.concatenate([bdiag(u16[i]), vbd[i]], axis=0)) for i in range(n)]
        upd = []
        pcol = []
        for i in range(n):
            ubd = bdiag(u16[i])
            tk = jnp.transpose(jnp.concatenate([bh[i], kh[i]], axis=0)).astype(BF16)
            lhs = jnp.concatenate([tk[:hd], tk[hd:]], axis=1)
            upd.append(_dot(lhs, jnp.concatenate([ubd[:lc], vbd[i][:lc], ubd[lc:], vbd[i][lc:]], axis=0)))
            tp = jnp.transpose(jnp.broadcast_to(pt[i], (pw, pw)))
            pcol.append(jnp.where(m0, tp[:hd], tp[hd:]))
        for i in range(n):
            y_ref[grp[i][0], :, sls[i]] = y[i]
            s_ref[sidx[i]] = s[i] * pcol[i] + upd[i]


def _wkv(r, w, k, v, kk, b):
    bsz, t_len, d = r.shape
    npairs = d // PAIR
    spec = pl.BlockSpec((bsz, WKV_CHUNK, d), lambda t: (0, t, 0))
    return pl.pallas_call(
        functools.partial(_wkv_kernel, nb=bsz, npairs=npairs),
        grid=(t_len // WKV_CHUNK,),
        in_specs=[spec] * 6,
        out_specs=spec,
        out_shape=jax.ShapeDtypeStruct((bsz, t_len, d), F32),
        scratch_shapes=[pltpu.VMEM((bsz * npairs, HEAD_DIM, PAIR), F32)],
        compiler_params=_params(),
        name="wkv7",
    )(r, w, k, v, kk, b)


def _rwkv_out_kernel(y_ref, bonus_ref, gate_ref, x_ref, lng_ref, lnb_ref, wo_ref, ind_ref, indt_ref,
                     g_ref, w1_ref, w2_ref, fg_ref, o_ref, *, final_norm):
    hr = y_ref.shape[0] // 2
    rsl = [slice(i * hr, (i + 1) * hr) for i in range(2)]
    y = [y_ref[r, :] for r in rsl]
    s1 = [_head_sum1(v, ind_ref) for v in y]
    mu = [_head_sum2(v, indt_ref) * (1.0 / HEAD_DIM) for v in s1]
    yc = [a - b for a, b in zip(y, mu)]
    s1 = [_head_sum1(v * v, ind_ref) for v in yc]
    var = [_head_sum2(v, indt_ref) * (1.0 / HEAD_DIM) for v in s1]
    yn = [a * lax.rsqrt(b + GN_EPS) * lng_ref[...] + lnb_ref[...] for a, b in zip(yc, var)]
    out = [((a + bonus_ref[r, :]) * gate_ref[r, :]).astype(BF16) for a, r in zip(yn, rsl)]
    x = jnp.concatenate([x_ref[r, :] + _dot(a, wo_ref[...]) for a, r in zip(out, rsl)], axis=0)
    res = _ffn_body(x, g_ref[...], w1_ref, w2_ref)
    if final_norm:
        res = _rms(res, fg_ref[...])
    o_ref[...] = res


def _rwkv_out(y, bonus, gate, x, p, g, w1, w2, ffn_layer, fg, final_norm):
    m = x.shape[0]
    tm = TM_OUT
    vec = _const_spec((1, D_MODEL))
    return pl.pallas_call(
        functools.partial(_rwkv_out_kernel, final_norm=final_norm),
        grid=(m // tm,),
        in_specs=[_row_spec(tm)] * 4 + [vec, vec, _layer_spec(p["w_o"], p["layer"]),
                                        _const_spec(p["ind"].shape), _const_spec(p["indt"].shape), vec,
                                        _layer_spec(w1, ffn_layer), _layer_spec(w2, ffn_layer), vec],
        out_specs=_row_spec(tm),
        out_shape=jax.ShapeDtypeStruct((m, D_MODEL), F32),
        compiler_params=_params(),
        name="rwkv_out_ffn",
    )(y, bonus, gate, x, p["ln_g"], p["ln_b"], p["w_o"], p["ind"], p["indt"], g, w1, w2, fg)


def kernel(x, norm_mix_g, norm_ffn_g, final_norm_g, ffn_w1, ffn_w2, sgu_w_in, sgu_b_in, sgu_ln_g, sgu_ln_b, sgu_w_s, sgu_b_s, sgu_w_out, rwkv_mu, rwkv_w_rkv, rwkv_w0, rwkv_w_lora_a, rwkv_w_lora_b, rwkv_a0, rwkv_a_lora_a, rwkv_a_lora_b, rwkv_v0, rwkv_v_lora_a, rwkv_v_lora_b, rwkv_g_lora_a, rwkv_g_lora_b, rwkv_k_k, rwkv_k_a, rwkv_r_k, rwkv_ln_g, rwkv_ln_b, rwkv_w_o):
    bsz, t_len, d = x.shape
    m = bsz * t_len
    row = lambda a: a.reshape(1, -1)
    head_of_lane = jnp.arange(d) // HEAD_DIM
    ind = (head_of_lane[:, None] == jnp.arange(LANES)[None, :]).astype(BF16)
    indt = jnp.concatenate([ind.T, ind.T], axis=0)
    xf = x.reshape(m, d)
    w1, w2 = ffn_w1.astype(BF16), ffn_w2.astype(BF16)
    w_in, w_out = sgu_w_in.astype(BF16), sgu_w_out.astype(BF16)
    w_rkv, w_o = rwkv_w_rkv.astype(BF16), rwkv_w_o.astype(BF16)
    v_first = None
    for i in range(DEPTH):
        j = i // 2
        if i % 2 == 0:
            xf = _sgu(xf, row(norm_mix_g[i]), w_in, row(sgu_b_in[j]), row(sgu_ln_g[j]),
                      row(sgu_ln_b[j]), sgu_w_s[j], sgu_b_s[j][:, :, None], w_out, j)
            xf = _ffn(xf, row(norm_ffn_g[i]), w1, w2, i)
        else:
            p = dict(g=row(norm_mix_g[i]), mu=rwkv_mu[j], w_rkv=w_rkv, layer=j, w0=row(rwkv_w0[j]),
                     w_la=rwkv_w_lora_a[j].astype(BF16), w_lb=rwkv_w_lora_b[j].astype(BF16),
                     a0=row(rwkv_a0[j]), a_la=rwkv_a_lora_a[j].astype(BF16), a_lb=rwkv_a_lora_b[j].astype(BF16),
                     g_la=rwkv_g_lora_a[j].astype(BF16), g_lb=rwkv_g_lora_b[j].astype(BF16),
                     k_k=row(rwkv_k_k[j]), k_a=row(rwkv_k_a[j]), r_k=row(rwkv_r_k[j]),
                     ln_g=row(rwkv_ln_g[j]), ln_b=row(rwkv_ln_b[j]), w_o=w_o,
                     ind=ind, indt=indt)
            if j > 0:
                p.update(v0=row(rwkv_v0[j - 1]), v_la=rwkv_v_lora_a[j - 1].astype(BF16),
                         v_lb=rwkv_v_lora_b[j - 1].astype(BF16))
            r, w, k, v, kk, b, gate, bonus = _rwkv_in(xf, v_first if j > 0 else None, p, t_len)
            if j == 0:
                v_first = v
            sh = (bsz, t_len, d)
            y = _wkv(r.reshape(sh), w.reshape(sh), k.reshape(sh), v.reshape(sh), kk.reshape(sh), b.reshape(sh))
            xf = _rwkv_out(y.reshape(m, d), bonus, gate, xf, p, row(norm_ffn_g[i]), w1, w2, i, row(final_norm_g),
                           i == DEPTH - 1)
    return xf.reshape(bsz, t_len, d)
```

```python
import functools

import jax
import jax.numpy as jnp
from jax import lax
from jax.experimental import pallas as pl
from jax.experimental.pallas import tpu as pltpu

F32 = jnp.float32
BF16 = jnp.bfloat16

D_MODEL = 1024
DEPTH = 4
CHUNK = 64
SGU_BLOCK = 128
SGU_WIDTH = 2 * D_MODEL
SGU_GROUPS = 16
SGU_GROUP_DIM = SGU_WIDTH // SGU_GROUPS
HEAD_DIM = 64
HEADS = D_MODEL // HEAD_DIM
FFN_HIDDEN = 4 * D_MODEL
RMS_EPS = 1e-6
LN_EPS = 1e-5
GN_EPS = 64e-5

LANES = 128
WKV_CHUNK = 64
PAIR = 2 * HEAD_DIM
WKV_GROUP = 16
WKV_CHUNKS_PER_STEP = 4

TM_FFN = 512
TM_SGU = 512
TM_RWKV = 512
TM_OUT = 512
VMEM_LIMIT = 56 * 1024 * 1024


def _dot(a, b):
    return jnp.dot(a, b, preferred_element_type=F32)


def _dot_nt(a, b):
    return lax.dot_general(a, b, (((1,), (1,)), ((), ())), preferred_element_type=F32)


def _split3(x):
    hi = x.astype(BF16)
    r1 = x - hi.astype(F32)
    mid = r1.astype(BF16)
    lo = (r1 - mid.astype(F32)).astype(BF16)
    return hi, mid, lo


def _rms(x, g):
    return x * lax.rsqrt(jnp.mean(x * x, axis=-1, keepdims=True) + RMS_EPS) * g


def _gelu(x):
    return 0.5 * x * (1.0 + lax.erf(x * 0.7071067811865476))


def _sigmoid(x):
    return 1.0 / (1.0 + jnp.exp(-x))


def _const_spec(shape):
    nd = len(shape)
    return pl.BlockSpec(shape, lambda i: (0,) * nd, pipeline_mode=pl.Buffered(1))


def _layer_spec(stacked, layer):
    shape = stacked.shape[1:]
    return pl.BlockSpec((None,) + shape, lambda i: (layer,) + (0,) * len(shape), pipeline_mode=pl.Buffered(1))


def _row_spec(tm, width=D_MODEL):
    return pl.BlockSpec((tm, width), lambda i: (i, 0))


def _params():
    return pltpu.CompilerParams(dimension_semantics=("arbitrary",), vmem_limit_bytes=VMEM_LIMIT)


def _ffn_body(x, g, w1_ref, w2_ref):
    xn = _rms(x, g).astype(BF16)
    acc = x
    for c in range(FFN_HIDDEN // D_MODEL):
        sl = slice(c * D_MODEL, (c + 1) * D_MODEL)
        h = jnp.maximum(_dot(xn, w1_ref[:, sl]), 0.0)
        acc = acc + _dot((h * h).astype(BF16), w2_ref[sl, :])
    return acc


def _ffn_kernel(x_ref, g_ref, w1_ref, w2_ref, o_ref):
    o_ref[...] = _ffn_body(x_ref[...], g_ref[...], w1_ref, w2_ref)


def _ffn(x, g, w1, w2, layer):
    m = x.shape[0]
    return pl.pallas_call(
        _ffn_kernel,
        grid=(m // TM_FFN,),
        in_specs=[_row_spec(TM_FFN), _const_spec((1, D_MODEL)), _layer_spec(w1, layer), _layer_spec(w2, layer)],
        out_specs=_row_spec(TM_FFN),
        out_shape=jax.ShapeDtypeStruct((m, D_MODEL), F32),
        compiler_params=_params(),
        name="ffn",
    )(x, g, w1, w2)


def _sgu_kernel(x_ref, g_ref, win_ref, bin_ref, lng_ref, lnb_ref, ws_ref, bs_ref, wout_ref, o_ref,
                vn_ref, y_ref, u_ref):
    x = x_ref[...]
    xn = _rms(x, g_ref[...]).astype(BF16)
    e = SGU_WIDTH
    zv = _gelu(_dot(xn, win_ref[:, e:]) + bin_ref[:, e:])
    u_ref[...] = _gelu(_dot(xn, win_ref[:, :e]) + bin_ref[:, :e])
    mu = jnp.mean(zv, axis=-1, keepdims=True)
    zc = zv - mu
    var = jnp.mean(zc * zc, axis=-1, keepdims=True)
    vn_ref[...] = (zc * lax.rsqrt(var + LN_EPS) * lng_ref[...] + lnb_ref[...]).astype(BF16)
    nblk = TM_SGU // SGU_BLOCK
    qi = lax.broadcasted_iota(jnp.int32, (SGU_BLOCK, SGU_BLOCK), 0) // CHUNK
    kj = lax.broadcasted_iota(jnp.int32, (SGU_BLOCK, SGU_BLOCK), 1) // CHUNK
    causal = kj <= qi
    gw = SGU_GROUP_DIM
    for gi in range(SGU_GROUPS):
        cs = slice(gi * gw, (gi + 1) * gw)
        ws = jnp.where(causal, ws_ref[gi], 0.0).astype(BF16)
        rhs = jnp.concatenate([vn_ref[n * SGU_BLOCK:(n + 1) * SGU_BLOCK, cs] for n in range(nblk)], axis=1)
        s = _dot(ws, rhs) + bs_ref[gi]
        for n in range(nblk):
            rs = slice(n * SGU_BLOCK, (n + 1) * SGU_BLOCK)
            y_ref[rs, cs] = (u_ref[rs, cs] * s[:, n * gw:(n + 1) * gw]).astype(BF16)
    o_ref[...] = x + _dot(y_ref[...], wout_ref[...])


def _sgu(x, g, w_in, b_in, ln_g, ln_b, w_s, b_s, w_out, layer):
    m = x.shape[0]
    e = SGU_WIDTH
    return pl.pallas_call(
        _sgu_kernel,
        grid=(m // TM_SGU,),
        in_specs=[_row_spec(TM_SGU), _const_spec((1, D_MODEL)), _layer_spec(w_in, layer),
                  _const_spec((1, 2 * e)), _const_spec((1, e)), _const_spec((1, e)),
                  _const_spec((SGU_GROUPS, SGU_BLOCK, SGU_BLOCK)), _const_spec((SGU_GROUPS, SGU_BLOCK, 1)),
                  _layer_spec(w_out, layer)],
        out_specs=_row_spec(TM_SGU),
        out_shape=jax.ShapeDtypeStruct((m, D_MODEL), F32),
        scratch_shapes=[pltpu.VMEM((TM_SGU, e), BF16), pltpu.VMEM((TM_SGU, e), BF16), pltpu.VMEM((TM_SGU, e), F32)],
        compiler_params=_params(),
        name="sgu",
    )(x, g, w_in, b_in, ln_g, ln_b, w_s, b_s, w_out)


def _head_sum1(x, ind_ref):
    return _dot(x.astype(BF16), ind_ref[...])


def _head_sum2(s, indt_ref):
    hi = s.astype(BF16)
    lo = (s - hi.astype(F32)).astype(BF16)
    return _dot(jnp.concatenate([hi, lo], axis=1), indt_ref[...])


def _rwkv_in_kernel(*refs, tiles_per_seq, has_vmix):
    if has_vmix:
        (x_ref, xp_ref, vf_ref, g_ref, mu_ref, wrkv_ref, w0_ref, wla_ref, wlb_ref, a0_ref, ala_ref, alb_ref,
         v0_ref, vla_ref, vlb_ref, gla_ref, glb_ref, kk_ref, ka_ref, rk_ref, ind_ref, indt_ref,
         r_o, w_o, k_o, v_o, kk_o, b_o, g_o, bonus_o) = refs
    else:
        (x_ref, xp_ref, g_ref, mu_ref, wrkv_ref, w0_ref, wla_ref, wlb_ref, a0_ref, ala_ref, alb_ref,
         gla_ref, glb_ref, kk_ref, ka_ref, rk_ref, ind_ref, indt_ref,
         r_o, w_o, k_o, v_o, kk_o, b_o, g_o, bonus_o) = refs
    i = pl.program_id(0)
    gn = g_ref[...]
    h = _rms(x_ref[...], gn)
    hp = _rms(xp_ref[7:8, :], gn)
    hp = jnp.where(i % tiles_per_seq == 0, 0.0, hp)
    row = lax.broadcasted_iota(jnp.int32, h.shape, 0)
    hs = jnp.where(row == 0, hp, pltpu.roll(h, 1, 0))
    xx = hs - h

    def mix(p):
        return (h + xx * mu_ref[p:p + 1, :]).astype(BF16)

    xv = mix(2)
    tw = _dot(mix(3), wla_ref[...])
    ta = _dot(mix(4), ala_ref[...])
    tg = _dot(mix(5), gla_ref[...])
    if has_vmix:
        tv = _dot(xv, vla_ref[...])
    k = _dot(mix(1), wrkv_ref[1])
    r = _dot(mix(0), wrkv_ref[0])
    kk = k * kk_ref[...]
    ss = _head_sum1(kk * kk, ind_ref)
    lw = w0_ref[...] + _dot(jnp.tanh(tw).astype(BF16), wlb_ref[...])
    a = _sigmoid(a0_ref[...] + _dot(ta.astype(BF16), alb_ref[...]))
    nrm = jnp.sqrt(_head_sum2(ss, indt_ref))
    v = _dot(xv, wrkv_ref[2])
    r_o[...] = r
    k = k * (1.0 + (a - 1.0) * ka_ref[...])
    rk = _head_sum1(r * k * rk_ref[...], ind_ref)
    g_o[...] = _dot(_sigmoid(tg).astype(BF16), glb_ref[...]).astype(BF16)
    if has_vmix:
        mixv = _sigmoid(v0_ref[...] + _dot(tv.astype(BF16), vlb_ref[...]))
        v = v + (vf_ref[...] - v) * mixv
    rk = _head_sum2(rk, indt_ref)
    w_o[...] = -0.6065306597126334 * _sigmoid(lw)
    k_o[...] = k
    v_o[...] = v
    kk = kk / jnp.maximum(nrm, 1e-12)
    kk_o[...] = kk
    b_o[...] = kk * a
    bonus_o[...] = (rk * v).astype(BF16)


def _rwkv_in(x, v_first, p, t_len):
    m = x.shape[0]
    tm = TM_RWKV
    has_vmix = v_first is not None
    prev_spec = pl.BlockSpec((8, D_MODEL), lambda i: (jnp.maximum(i * (tm // 8) - 1, 0), 0))
    ins = [x, x] + ([v_first] if has_vmix else [])
    specs = [_row_spec(tm), prev_spec] + ([_row_spec(tm)] if has_vmix else [])

    def add(a, spec=None):
        ins.append(a)
        specs.append(spec if spec is not None else _const_spec(a.shape))

    add(p["g"]); add(p["mu"]); add(p["w_rkv"], _layer_spec(p["w_rkv"], p["layer"]))
    add(p["w0"]); add(p["w_la"]); add(p["w_lb"])
    add(p["a0"]); add(p["a_la"]); add(p["a_lb"])
    if has_vmix:
        add(p["v0"]); add(p["v_la"]); add(p["v_lb"])
    add(p["g_la"]); add(p["g_lb"]); add(p["k_k"]); add(p["k_a"]); add(p["r_k"]); add(p["ind"]); add(p["indt"])
    out = jax.ShapeDtypeStruct((m, D_MODEL), F32)
    half = jax.ShapeDtypeStruct((m, D_MODEL), BF16)
    return pl.pallas_call(
        functools.partial(_rwkv_in_kernel, tiles_per_seq=t_len // tm, has_vmix=has_vmix),
        grid=(m // tm,),
        in_specs=specs,
        out_specs=[_row_spec(tm)] * 8,
        out_shape=[out] * 6 + [half] * 2,
        compiler_params=_params(),
        name="rwkv_in",
    )(*ins)


def _wkv_kernel(r_ref, w_ref, k_ref, v_ref, kk_ref, b_ref, y_ref, s_ref, *, nb, npairs):
    lc, hd, pw = WKV_CHUNK, HEAD_DIM, PAIR

    @pl.when(pl.program_id(0) == 0)
    def _():
        s_ref[...] = jnp.zeros_like(s_ref)

    lane = lax.broadcasted_iota(jnp.int32, (lc, pw), 1)
    row = lax.broadcasted_iota(jnp.int32, (lc, pw), 0)
    m0 = lane < hd
    col = jnp.where(m0, lane, lane - hd)
    strict = col < row
    incl = col <= row
    m0w = (lax.broadcasted_iota(jnp.int32, (lc, 2 * pw), 1) % pw) < hd
    tr = lax.broadcasted_iota(jnp.int32, (lc, 3 * lc), 0)
    tc = lax.broadcasted_iota(jnp.int32, (lc, 3 * lc), 1) % lc
    tri3 = jnp.where(tc <= tr, 1.0, 0.0).astype(BF16)

    def bdiag(x):
        z = jnp.zeros_like(x)
        return jnp.concatenate([jnp.where(m0, x, z), jnp.where(m0, z, x)], axis=0)

    for c in range(WKV_CHUNKS_PER_STEP):
        _wkv_chunk(slice(c * lc, (c + 1) * lc), r_ref, w_ref, k_ref, v_ref, kk_ref, b_ref, y_ref, s_ref,
                   nb, npairs, m0, strict, incl, m0w, tri3, bdiag)


def _wkv_chunk(ts, r_ref, w_ref, k_ref, v_ref, kk_ref, b_ref, y_ref, s_ref, nb, npairs, m0, strict, incl, m0w,
               tri3, bdiag):
    lc, hd, pw = WKV_CHUNK, HEAD_DIM, PAIR
    chains = []
    for bi in range(nb):
        w = w_ref[bi, ts, :]
        cum = _dot(tri3, jnp.concatenate(_split3(w), axis=0))
        tot = cum[lc - 1:lc, :]
        e_incl = jnp.exp(cum)
        e_excl = jnp.exp(cum - w)
        e_inv = jnp.exp(-cum)
        e_rem = jnp.exp(tot - cum)
        k = k_ref[bi, ts, :]
        b = b_ref[bi, ts, :]
        chains.extend((bi, p,
                       (-(kk_ref[bi, ts, :] * e_excl)).astype(BF16), (r_ref[bi, ts, :] * e_incl).astype(BF16),
                       (b * e_inv).astype(BF16), (k * e_inv).astype(BF16),
                       b * e_rem, k * e_rem,
                       v_ref[bi, ts, :].astype(BF16), jnp.exp(tot)) for p in range(npairs))

    for g0 in range(0, len(chains), WKV_GROUP):
        grp = chains[g0:g0 + WKV_GROUP]
        n = len(grp)
        sls = [slice(c[1] * pw, (c[1] + 1) * pw) for c in grp]
        at, rt, bt, kt, bh, kh, vp, pt = ([c[j][:, sl] for c, sl in zip(grp, sls)] for j in range(2, 10))
        sidx = [c[0] * npairs + c[1] for c in grp]
        ar = [jnp.concatenate([at[i], rt[i]], axis=0) for i in range(n)]
        g = [_dot_nt(ar[i], jnp.concatenate([bdiag(bt[i]), bdiag(kt[i])], axis=0)) for i in range(n)]
        nmat = [jnp.where(strict, g[i][:lc, :pw], 0.0) for i in range(n)]
        aak = [jnp.where(strict, g[i][:lc, pw:], 0.0).astype(BF16) for i in range(n)]
        arbk = [jnp.concatenate([jnp.where(incl, g[i][lc:, :pw], 0.0), jnp.where(incl, g[i][lc:, pw:], 0.0)],
                                axis=1).astype(BF16) for i in range(n)]
        s = [s_ref[sidx[i]] for i in range(n)]
        vbd = [bdiag(vp[i]) for i in range(n)]
        sbd = [bdiag(s[i].astype(BF16)) for i in range(n)]
        wmat = [_dot(jnp.concatenate([at[i], aak[i]], axis=1), jnp.concatenate([sbd[i], vbd[i]], axis=0))
                for i in range(n)]
        rs = [_dot(rt[i], sbd[i]) for i in range(n)]
        for it in range(6):
            if it < 5:
                o = []
                for i in range(n):
                    pm = jnp.concatenate([nmat[i], wmat[i]], axis=1).astype(BF16)
                    z = jnp.zeros_like(pm)
                    rhs = jnp.concatenate([jnp.where(m0w, pm, z), jnp.where(m0w, z, pm)], axis=0)
                    o.append(_dot(nmat[i].astype(BF16), rhs))
                wmat = [wmat[i] + o[i][:, pw:] for i in range(n)]
                nmat = [o[i][:, :pw] for i in range(n)]
            else:
                o = [_dot(nmat[i].astype(BF16), bdiag(wmat[i].astype(BF16))) for i in range(n)]
                wmat = [wmat[i] + o[i] for i in range(n)]
        u16 = [wmat[i].astype(BF16) for i in range(n)]
        y = [rs[i] + _dot(arbk[i], jnp.concatenate([bdiag(u16[i]), vbd[i]], axis=0)) for i in range(n)]
        upd = []
        pcol = []
        for i in range(n):
            ubd = bdiag(u16[i])
            tk = jnp.transpose(jnp.concatenate([bh[i], kh[i]], axis=0)).astype(BF16)
            lhs = jnp.concatenate([tk[:hd], tk[hd:]], axis=1)
            upd.append(_dot(lhs, jnp.concatenate([ubd[:lc], vbd[i][:lc], ubd[lc:], vbd[i][lc:]], axis=0)))
            tp = jnp.transpose(jnp.broadcast_to(pt[i], (pw, pw)))
            pcol.append(jnp.where(m0, tp[:hd], tp[hd:]))
        for i in range(n):
            y_ref[grp[i][0], ts, sls[i]] = y[i]
            s_ref[sidx[i]] = s[i] * pcol[i] + upd[i]


def _wkv(r, w, k, v, kk, b):
    bsz, t_len, d = r.shape
    npairs = d // PAIR
    t_blk = WKV_CHUNKS_PER_STEP * WKV_CHUNK
    spec = pl.BlockSpec((bsz, t_blk, d), lambda t: (0, t, 0))
    return pl.pallas_call(
        functools.partial(_wkv_kernel, nb=bsz, npairs=npairs),
        grid=(t_len // t_blk,),
        in_specs=[spec] * 6,
        out_specs=spec,
        out_shape=jax.ShapeDtypeStruct((bsz, t_len, d), F32),
        scratch_shapes=[pltpu.VMEM((bsz * npairs, HEAD_DIM, PAIR), F32)],
        compiler_params=_params(),
        name="wkv7",
    )(r, w, k, v, kk, b)


def _rwkv_out_kernel(y_ref, bonus_ref, gate_ref, x_ref, lng_ref, lnb_ref, wo_ref, ind_ref, indt_ref,
                     g_ref, w1_ref, w2_ref, fg_ref, o_ref, *, final_norm):
    hr = y_ref.shape[0] // 2
    rsl = [slice(i * hr, (i + 1) * hr) for i in range(2)]
    y = [y_ref[r, :] for r in rsl]
    s1 = [_head_sum1(v, ind_ref) for v in y]
    mu = [_head_sum2(v, indt_ref) * (1.0 / HEAD_DIM) for v in s1]
    yc = [a - b for a, b in zip(y, mu)]
    s1 = [_head_sum1(v * v, ind_ref) for v in yc]
    var = [_head_sum2(v, indt_ref) * (1.0 / HEAD_DIM) for v in s1]
    yn = [a * lax.rsqrt(b + GN_EPS) * lng_ref[...] + lnb_ref[...] for a, b in zip(yc, var)]
    out = [((a + bonus_ref[r, :]) * gate_ref[r, :]).astype(BF16) for a, r in zip(yn, rsl)]
    x = jnp.concatenate([x_ref[r, :] + _dot(a, wo_ref[...]) for a, r in zip(out, rsl)], axis=0)
    res = _ffn_body(x, g_ref[...], w1_ref, w2_ref)
    if final_norm:
        res = _rms(res, fg_ref[...])
    o_ref[...] = res


def _rwkv_out(y, bonus, gate, x, p, g, w1, w2, ffn_layer, fg, final_norm):
    m = x.shape[0]
    tm = TM_OUT
    vec = _const_spec((1, D_MODEL))
    return pl.pallas_call(
        functools.partial(_rwkv_out_kernel, final_norm=final_norm),
        grid=(m // tm,),
        in_specs=[_row_spec(tm)] * 4 + [vec, vec, _layer_spec(p["w_o"], p["layer"]),
                                        _const_spec(p["ind"].shape), _const_spec(p["indt"].shape), vec,
                                        _layer_spec(w1, ffn_layer), _layer_spec(w2, ffn_layer), vec],
        out_specs=_row_spec(tm),
        out_shape=jax.ShapeDtypeStruct((m, D_MODEL), F32),
        compiler_params=_params(),
        name="rwkv_out_ffn",
    )(y, bonus, gate, x, p["ln_g"], p["ln_b"], p["w_o"], p["ind"], p["indt"], g, w1, w2, fg)


def kernel(x, norm_mix_g, norm_ffn_g, final_norm_g, ffn_w1, ffn_w2, sgu_w_in, sgu_b_in, sgu_ln_g, sgu_ln_b, sgu_w_s, sgu_b_s, sgu_w_out, rwkv_mu, rwkv_w_rkv, rwkv_w0, rwkv_w_lora_a, rwkv_w_lora_b, rwkv_a0, rwkv_a_lora_a, rwkv_a_lora_b, rwkv_v0, rwkv_v_lora_a, rwkv_v_lora_b, rwkv_g_lora_a, rwkv_g_lora_b, rwkv_k_k, rwkv_k_a, rwkv_r_k, rwkv_ln_g, rwkv_ln_b, rwkv_w_o):
    bsz, t_len, d = x.shape
    m = bsz * t_len
    row = lambda a: a.reshape(1, -1)
    head_of_lane = jnp.arange(d) // HEAD_DIM
    ind = (head_of_lane[:, None] == jnp.arange(LANES)[None, :]).astype(BF16)
    indt = jnp.concatenate([ind.T, ind.T], axis=0)
    xf = x.reshape(m, d)
    w1, w2 = ffn_w1.astype(BF16), ffn_w2.astype(BF16)
    w_in, w_out = sgu_w_in.astype(BF16), sgu_w_out.astype(BF16)
    w_rkv, w_o = rwkv_w_rkv.astype(BF16), rwkv_w_o.astype(BF16)
    v_first = None
    for i in range(DEPTH):
        j = i // 2
        if i % 2 == 0:
            xf = _sgu(xf, row(norm_mix_g[i]), w_in, row(sgu_b_in[j]), row(sgu_ln_g[j]),
                      row(sgu_ln_b[j]), sgu_w_s[j], sgu_b_s[j][:, :, None], w_out, j)
            xf = _ffn(xf, row(norm_ffn_g[i]), w1, w2, i)
        else:
            p = dict(g=row(norm_mix_g[i]), mu=rwkv_mu[j], w_rkv=w_rkv, layer=j, w0=row(rwkv_w0[j]),
                     w_la=rwkv_w_lora_a[j].astype(BF16), w_lb=rwkv_w_lora_b[j].astype(BF16),
                     a0=row(rwkv_a0[j]), a_la=rwkv_a_lora_a[j].astype(BF16), a_lb=rwkv_a_lora_b[j].astype(BF16),
                     g_la=rwkv_g_lora_a[j].astype(BF16), g_lb=rwkv_g_lora_b[j].astype(BF16),
                     k_k=row(rwkv_k_k[j]), k_a=row(rwkv_k_a[j]), r_k=row(rwkv_r_k[j]),
                     ln_g=row(rwkv_ln_g[j]), ln_b=row(rwkv_ln_b[j]), w_o=w_o,
                     ind=ind, indt=indt)
            if j > 0:
                p.update(v0=row(rwkv_v0[j - 1]), v_la=rwkv_v_lora_a[j - 1].astype(BF16),
                         v_lb=rwkv_v_lora_b[j - 1].astype(BF16))
            r, w, k, v, kk, b, gate, bonus = _rwkv_in(xf, v_first if j > 0 else None, p, t_len)
            if j == 0:
                v_first = v
            sh = (bsz, t_len, d)
            y = _wkv(r.reshape(sh), w.reshape(sh), k.reshape(sh), v.reshape(sh), kk.reshape(sh), b.reshape(sh))
            xf = _rwkv_out(y.reshape(m, d), bonus, gate, xf, p, row(norm_ffn_g[i]), w1, w2, i, row(final_norm_g),
                           i == DEPTH - 1)
    return xf.reshape(bsz, t_len, d)
```

```python
import functools

import jax
import jax.numpy as jnp
from jax import lax
from jax.experimental import pallas as pl
from jax.experimental.pallas import tpu as pltpu

F32 = jnp.float32
BF16 = jnp.bfloat16

D_MODEL = 1024
DEPTH = 4
CHUNK = 64
SGU_BLOCK = 128
SGU_WIDTH = 2 * D_MODEL
SGU_GROUPS = 16
SGU_GROUP_DIM = SGU_WIDTH // SGU_GROUPS
HEAD_DIM = 64
HEADS = D_MODEL // HEAD_DIM
FFN_HIDDEN = 4 * D_MODEL
RMS_EPS = 1e-6
LN_EPS = 1e-5
GN_EPS = 64e-5

LANES = 128
WKV_CHUNK = 64
PAIR = 2 * HEAD_DIM
WKV_GROUP = 16
WKV_CHUNKS_PER_STEP = 4

TM_FFN = 512
TM_SGU = 512
TM_RWKV = 512
TM_OUT = 512
VMEM_LIMIT = 56 * 1024 * 1024


def _dot(a, b):
    return jnp.dot(a, b, preferred_element_type=F32)


def _dot_nt(a, b):
    return lax.dot_general(a, b, (((1,), (1,)), ((), ())), preferred_element_type=F32)


def _split3(x):
    hi = x.astype(BF16)
    r1 = x - hi.astype(F32)
    mid = r1.astype(BF16)
    lo = (r1 - mid.astype(F32)).astype(BF16)
    return hi, mid, lo


def _rms(x, g):
    return x * lax.rsqrt(jnp.mean(x * x, axis=-1, keepdims=True) + RMS_EPS) * g


def _gelu(x):
    return 0.5 * x * (1.0 + lax.erf(x * 0.7071067811865476))


def _sigmoid(x):
    return 1.0 / (1.0 + jnp.exp(-x))


def _const_spec(shape):
    nd = len(shape)
    return pl.BlockSpec(shape, lambda i: (0,) * nd, pipeline_mode=pl.Buffered(1))


def _layer_spec(stacked, layer):
    shape = stacked.shape[1:]
    return pl.BlockSpec((None,) + shape, lambda i: (layer,) + (0,) * len(shape), pipeline_mode=pl.Buffered(1))


def _row_spec(tm, width=D_MODEL):
    return pl.BlockSpec((tm, width), lambda i: (i, 0))


def _params():
    return pltpu.CompilerParams(dimension_semantics=("arbitrary",), vmem_limit_bytes=VMEM_LIMIT)


def _ffn_body(x, g, w1_ref, w2_ref):
    xn = _rms(x, g).astype(BF16)
    acc = x
    for c in range(FFN_HIDDEN // D_MODEL):
        sl = slice(c * D_MODEL, (c + 1) * D_MODEL)
        h = jnp.maximum(_dot(xn, w1_ref[:, sl]), 0.0)
        acc = acc + _dot((h * h).astype(BF16), w2_ref[sl, :])
    return acc


def _ffn_kernel(x_ref, g_ref, w1_ref, w2_ref, o_ref):
    o_ref[...] = _ffn_body(x_ref[...], g_ref[...], w1_ref, w2_ref)


def _ffn(x, g, w1, w2, layer):
    m = x.shape[0]
    return pl.pallas_call(
        _ffn_kernel,
        grid=(m // TM_FFN,),
        in_specs=[_row_spec(TM_FFN), _const_spec((1, D_MODEL)), _layer_spec(w1, layer), _layer_spec(w2, layer)],
        out_specs=_row_spec(TM_FFN),
        out_shape=jax.ShapeDtypeStruct((m, D_MODEL), F32),
        compiler_params=_params(),
        name="ffn",
    )(x, g, w1, w2)


def _sgu_kernel(x_ref, g_ref, win_ref, bin_ref, lng_ref, lnb_ref, ws_ref, bs_ref, wout_ref, o_ref,
                vn_ref, y_ref, u_ref):
    x = x_ref[...]
    xn = _rms(x, g_ref[...]).astype(BF16)
    e = SGU_WIDTH
    zv = _gelu(_dot(xn, win_ref[:, e:]) + bin_ref[:, e:])
    u_ref[...] = _gelu(_dot(xn, win_ref[:, :e]) + bin_ref[:, :e])
    mu = jnp.mean(zv, axis=-1, keepdims=True)
    zc = zv - mu
    var = jnp.mean(zc * zc, axis=-1, keepdims=True)
    vn_ref[...] = (zc * lax.rsqrt(var + LN_EPS) * lng_ref[...] + lnb_ref[...]).astype(BF16)
    nblk = TM_SGU // SGU_BLOCK
    qi = lax.broadcasted_iota(jnp.int32, (SGU_BLOCK, SGU_BLOCK), 0) // CHUNK
    kj = lax.broadcasted_iota(jnp.int32, (SGU_BLOCK, SGU_BLOCK), 1) // CHUNK
    causal = kj <= qi
    gw = SGU_GROUP_DIM
    for gi in range(SGU_GROUPS):
        cs = slice(gi * gw, (gi + 1) * gw)
        ws = jnp.where(causal, ws_ref[gi], 0.0).astype(BF16)
        rhs = jnp.concatenate([vn_ref[n * SGU_BLOCK:(n + 1) * SGU_BLOCK, cs] for n in range(nblk)], axis=1)
        s = _dot(ws, rhs) + bs_ref[gi]
        for n in range(nblk):
            rs = slice(n * SGU_BLOCK, (n + 1) * SGU_BLOCK)
            y_ref[rs, cs] = (u_ref[rs, cs] * s[:, n * gw:(n + 1) * gw]).astype(BF16)
    o_ref[...] = x + _dot(y_ref[...], wout_ref[...])


def _sgu(x, g, w_in, b_in, ln_g, ln_b, w_s, b_s, w_out, layer):
    m = x.shape[0]
    e = SGU_WIDTH
    return pl.pallas_call(
        _sgu_kernel,
        grid=(m // TM_SGU,),
        in_specs=[_row_spec(TM_SGU), _const_spec((1, D_MODEL)), _layer_spec(w_in, layer),
                  _const_spec((1, 2 * e)), _const_spec((1, e)), _const_spec((1, e)),
                  _const_spec((SGU_GROUPS, SGU_BLOCK, SGU_BLOCK)), _const_spec((SGU_GROUPS, SGU_BLOCK, 1)),
                  _layer_spec(w_out, layer)],
        out_specs=_row_spec(TM_SGU),
        out_shape=jax.ShapeDtypeStruct((m, D_MODEL), F32),
        scratch_shapes=[pltpu.VMEM((TM_SGU, e), BF16), pltpu.VMEM((TM_SGU, e), BF16), pltpu.VMEM((TM_SGU, e), F32)],
        compiler_params=_params(),
        name="sgu",
    )(x, g, w_in, b_in, ln_g, ln_b, w_s, b_s, w_out)


def _head_sum(x, ones_ref):
    tm, nt = x.shape[0], x.shape[1] // LANES
    stacked = jnp.concatenate([x[:, j * LANES:(j + 1) * LANES] for j in range(nt)], axis=0).astype(BF16)
    s = _dot(stacked, ones_ref[...])
    return jnp.concatenate([s[j * tm:(j + 1) * tm] for j in range(nt)], axis=1)


def _rwkv_in_kernel(*refs, tiles_per_seq, has_vmix):
    if has_vmix:
        (x_ref, xp_ref, vf_ref, g_ref, mu_ref, wrkv_ref, w0_ref, wla_ref, wlb_ref, a0_ref, ala_ref, alb_ref,
         v0_ref, vla_ref, vlb_ref, gla_ref, glb_ref, kk_ref, ka_ref, rk_ref, ones_ref,
         r_o, w_o, k_o, v_o, kk_o, b_o, g_o, bonus_o) = refs
    else:
        (x_ref, xp_ref, g_ref, mu_ref, wrkv_ref, w0_ref, wla_ref, wlb_ref, a0_ref, ala_ref, alb_ref,
         gla_ref, glb_ref, kk_ref, ka_ref, rk_ref, ones_ref,
         r_o, w_o, k_o, v_o, kk_o, b_o, g_o, bonus_o) = refs
    i = pl.program_id(0)
    gn = g_ref[...]
    h = _rms(x_ref[...], gn)
    hp = _rms(xp_ref[7:8, :], gn)
    hp = jnp.where(i % tiles_per_seq == 0, 0.0, hp)
    row = lax.broadcasted_iota(jnp.int32, h.shape, 0)
    hs = jnp.where(row == 0, hp, pltpu.roll(h, 1, 0))
    xx = hs - h

    def mix(p):
        return (h + xx * mu_ref[p:p + 1, :]).astype(BF16)

    xv = mix(2)
    tw = _dot(mix(3), wla_ref[...])
    ta = _dot(mix(4), ala_ref[...])
    tg = _dot(mix(5), gla_ref[...])
    if has_vmix:
        tv = _dot(xv, vla_ref[...])
    k = _dot(mix(1), wrkv_ref[1])
    r = _dot(mix(0), wrkv_ref[0])
    kk = k * kk_ref[...]
    ss = _head_sum(kk * kk, ones_ref)
    lw = w0_ref[...] + _dot(jnp.tanh(tw).astype(BF16), wlb_ref[...])
    a = _sigmoid(a0_ref[...] + _dot(ta.astype(BF16), alb_ref[...]))
    nrm = jnp.sqrt(ss)
    v = _dot(xv, wrkv_ref[2])
    r_o[...] = r
    k = k * (1.0 + (a - 1.0) * ka_ref[...])
    rk = _head_sum(r * k * rk_ref[...], ones_ref)
    g_o[...] = _dot(_sigmoid(tg).astype(BF16), glb_ref[...]).astype(BF16)
    if has_vmix:
        mixv = _sigmoid(v0_ref[...] + _dot(tv.astype(BF16), vlb_ref[...]))
        v = v + (vf_ref[...] - v) * mixv
    w_o[...] = -0.6065306597126334 * _sigmoid(lw)
    k_o[...] = k
    v_o[...] = v
    kk = kk / jnp.maximum(nrm, 1e-12)
    kk_o[...] = kk
    b_o[...] = kk * a
    bonus_o[...] = (rk * v).astype(BF16)


def _rwkv_in(x, v_first, p, t_len):
    m = x.shape[0]
    tm = TM_RWKV
    has_vmix = v_first is not None
    prev_spec = pl.BlockSpec((8, D_MODEL), lambda i: (jnp.maximum(i * (tm // 8) - 1, 0), 0))
    ins = [x, x] + ([v_first] if has_vmix else [])
    specs = [_row_spec(tm), prev_spec] + ([_row_spec(tm)] if has_vmix else [])

    def add(a, spec=None):
        ins.append(a)
        specs.append(spec if spec is not None else _const_spec(a.shape))

    add(p["g"]); add(p["mu"]); add(p["w_rkv"], _layer_spec(p["w_rkv"], p["layer"]))
    add(p["w0"]); add(p["w_la"]); add(p["w_lb"])
    add(p["a0"]); add(p["a_la"]); add(p["a_lb"])
    if has_vmix:
        add(p["v0"]); add(p["v_la"]); add(p["v_lb"])
    add(p["g_la"]); add(p["g_lb"]); add(p["k_k"]); add(p["k_a"]); add(p["r_k"]); add(p["ones"])
    out = jax.ShapeDtypeStruct((m, D_MODEL), F32)
    half = jax.ShapeDtypeStruct((m, D_MODEL), BF16)
    return pl.pallas_call(
        functools.partial(_rwkv_in_kernel, tiles_per_seq=t_len // tm, has_vmix=has_vmix),
        grid=(m // tm,),
        in_specs=specs,
        out_specs=[_row_spec(tm)] * 8,
        out_shape=[out] * 6 + [half] * 2,
        compiler_params=_params(),
        name="rwkv_in",
    )(*ins)


def _wkv_kernel(r_ref, w_ref, k_ref, v_ref, kk_ref, b_ref, y_ref, s_ref, *, nb, npairs):
    lc, hd, pw = WKV_CHUNK, HEAD_DIM, PAIR

    @pl.when(pl.program_id(0) == 0)
    def _():
        s_ref[...] = jnp.zeros_like(s_ref)

    lane = lax.broadcasted_iota(jnp.int32, (lc, pw), 1)
    row = lax.broadcasted_iota(jnp.int32, (lc, pw), 0)
    m0 = lane < hd
    col = jnp.where(m0, lane, lane - hd)
    strict = col < row
    incl = col <= row
    m0w = (lax.broadcasted_iota(jnp.int32, (lc, 2 * pw), 1) % pw) < hd
    tr = lax.broadcasted_iota(jnp.int32, (lc, 3 * lc), 0)
    tc = lax.broadcasted_iota(jnp.int32, (lc, 3 * lc), 1) % lc
    tri3 = jnp.where(tc <= tr, 1.0, 0.0).astype(BF16)

    def bdiag(x):
        z = jnp.zeros_like(x)
        return jnp.concatenate([jnp.where(m0, x, z), jnp.where(m0, z, x)], axis=0)

    for c in range(WKV_CHUNKS_PER_STEP):
        _wkv_chunk(slice(c * lc, (c + 1) * lc), r_ref, w_ref, k_ref, v_ref, kk_ref, b_ref, y_ref, s_ref,
                   nb, npairs, m0, strict, incl, m0w, tri3, bdiag)


def _wkv_chunk(ts, r_ref, w_ref, k_ref, v_ref, kk_ref, b_ref, y_ref, s_ref, nb, npairs, m0, strict, incl, m0w,
               tri3, bdiag):
    lc, hd, pw = WKV_CHUNK, HEAD_DIM, PAIR
    chains = []
    for bi in range(nb):
        w = w_ref[bi, ts, :]
        cum = _dot(tri3, jnp.concatenate(_split3(w), axis=0))
        tot = cum[lc - 1:lc, :]
        e_incl = jnp.exp(cum)
        e_excl = jnp.exp(cum - w)
        e_inv = jnp.exp(-cum)
        e_rem = jnp.exp(tot - cum)
        k = k_ref[bi, ts, :]
        b = b_ref[bi, ts, :]
        chains.extend((bi, p,
                       (-(kk_ref[bi, ts, :] * e_excl)).astype(BF16), (r_ref[bi, ts, :] * e_incl).astype(BF16),
                       (b * e_inv).astype(BF16), (k * e_inv).astype(BF16),
                       b * e_rem, k * e_rem,
                       v_ref[bi, ts, :].astype(BF16), jnp.exp(tot)) for p in range(npairs))

    for g0 in range(0, len(chains), WKV_GROUP):
        grp = chains[g0:g0 + WKV_GROUP]
        n = len(grp)
        sls = [slice(c[1] * pw, (c[1] + 1) * pw) for c in grp]
        at, rt, bt, kt, bh, kh, vp, pt = ([c[j][:, sl] for c, sl in zip(grp, sls)] for j in range(2, 10))
        sidx = [c[0] * npairs + c[1] for c in grp]
        ar = [jnp.concatenate([at[i], rt[i]], axis=0) for i in range(n)]
        g = [_dot_nt(ar[i], jnp.concatenate([bdiag(bt[i]), bdiag(kt[i])], axis=0)) for i in range(n)]
        nmat = [jnp.where(strict, g[i][:lc, :pw], 0.0) for i in range(n)]
        aak = [jnp.where(strict, g[i][:lc, pw:], 0.0).astype(BF16) for i in range(n)]
        arbk = [jnp.concatenate([jnp.where(incl, g[i][lc:, :pw], 0.0), jnp.where(incl, g[i][lc:, pw:], 0.0)],
                                axis=1).astype(BF16) for i in range(n)]
        s = [s_ref[sidx[i]] for i in range(n)]
        vbd = [bdiag(vp[i]) for i in range(n)]
        sbd = [bdiag(s[i].astype(BF16)) for i in range(n)]
        wmat = [_dot(jnp.concatenate([at[i], aak[i]], axis=1), jnp.concatenate([sbd[i], vbd[i]], axis=0))
                for i in range(n)]
        rs = [_dot(rt[i], sbd[i]) for i in range(n)]
        for it in range(6):
            if it < 5:
                o = []
                for i in range(n):
                    pm = jnp.concatenate([nmat[i], wmat[i]], axis=1).astype(BF16)
                    z = jnp.zeros_like(pm)
                    rhs = jnp.concatenate([jnp.where(m0w, pm, z), jnp.where(m0w, z, pm)], axis=0)
                    o.append(_dot(nmat[i].astype(BF16), rhs))
                wmat = [wmat[i] + o[i][:, pw:] for i in range(n)]
                nmat = [o[i][:, :pw] for i in range(n)]
            else:
                o = [_dot(nmat[i].astype(BF16), bdiag(wmat[i].astype(BF16))) for i in range(n)]
                wmat = [wmat[i] + o[i] for i in range(n)]
        u16 = [wmat[i].astype(BF16) for i in range(n)]
        y = [rs[i] + _dot(arbk[i], jnp.concatenate([bdiag(u16[i]), vbd[i]], axis=0)) for i in range(n)]
        upd = []
        pcol = []
        for i in range(n):
            ubd = bdiag(u16[i])
            tk = jnp.transpose(jnp.concatenate([bh[i], kh[i]], axis=0)).astype(BF16)
            lhs = jnp.concatenate([tk[:hd], tk[hd:]], axis=1)
            upd.append(_dot(lhs, jnp.concatenate([ubd[:lc], vbd[i][:lc], ubd[lc:], vbd[i][lc:]], axis=0)))
            tp = jnp.transpose(jnp.broadcast_to(pt[i], (pw, pw)))
            pcol.append(jnp.where(m0, tp[:hd], tp[hd:]))
        for i in range(n):
            y_ref[grp[i][0], ts, sls[i]] = y[i]
            s_ref[sidx[i]] = s[i] * pcol[i] + upd[i]


def _wkv(r, w, k, v, kk, b):
    bsz, t_len, d = r.shape
    npairs = d // PAIR
    t_blk = WKV_CHUNKS_PER_STEP * WKV_CHUNK
    spec = pl.BlockSpec((bsz, t_blk, d), lambda t: (0, t, 0))
    return pl.pallas_call(
        functools.partial(_wkv_kernel, nb=bsz, npairs=npairs),
        grid=(t_len // t_blk,),
        in_specs=[spec] * 6,
        out_specs=spec,
        out_shape=jax.ShapeDtypeStruct((bsz, t_len, d), F32),
        scratch_shapes=[pltpu.VMEM((bsz * npairs, HEAD_DIM, PAIR), F32)],
        compiler_params=_params(),
        name="wkv7",
    )(r, w, k, v, kk, b)


def _rwkv_out_kernel(y_ref, bonus_ref, gate_ref, x_ref, lng_ref, lnb_ref, wo_ref, ones_ref,
                     g_ref, w1_ref, w2_ref, fg_ref, o_ref, *, final_norm):
    hr = y_ref.shape[0] // 2
    rsl = [slice(i * hr, (i + 1) * hr) for i in range(2)]
    y = [y_ref[r, :] for r in rsl]
    mu = [_head_sum(v, ones_ref) * (1.0 / HEAD_DIM) for v in y]
    yc = [a - b for a, b in zip(y, mu)]
    var = [_head_sum(v * v, ones_ref) * (1.0 / HEAD_DIM) for v in yc]
    yn = [a * lax.rsqrt(b + GN_EPS) * lng_ref[...] + lnb_ref[...] for a, b in zip(yc, var)]
    out = [((a + bonus_ref[r, :]) * gate_ref[r, :]).astype(BF16) for a, r in zip(yn, rsl)]
    x = jnp.concatenate([x_ref[r, :] + _dot(a, wo_ref[...]) for a, r in zip(out, rsl)], axis=0)
    res = _ffn_body(x, g_ref[...], w1_ref, w2_ref)
    if final_norm:
        res = _rms(res, fg_ref[...])
    o_ref[...] = res


def _rwkv_out(y, bonus, gate, x, p, g, w1, w2, ffn_layer, fg, final_norm):
    m = x.shape[0]
    tm = TM_OUT
    vec = _const_spec((1, D_MODEL))
    return pl.pallas_call(
        functools.partial(_rwkv_out_kernel, final_norm=final_norm),
        grid=(m // tm,),
        in_specs=[_row_spec(tm)] * 4 + [vec, vec, _layer_spec(p["w_o"], p["layer"]),
                                        _const_spec(p["ones"].shape), vec,
                                        _layer_spec(w1, ffn_layer), _layer_spec(w2, ffn_layer), vec],
        out_specs=_row_spec(tm),
        out_shape=jax.ShapeDtypeStruct((m, D_MODEL), F32),
        compiler_params=_params(),
        name="rwkv_out_ffn",
    )(y, bonus, gate, x, p["ln_g"], p["ln_b"], p["w_o"], p["ones"], g, w1, w2, fg)


def kernel(x, norm_mix_g, norm_ffn_g, final_norm_g, ffn_w1, ffn_w2, sgu_w_in, sgu_b_in, sgu_ln_g, sgu_ln_b, sgu_w_s, sgu_b_s, sgu_w_out, rwkv_mu, rwkv_w_rkv, rwkv_w0, rwkv_w_lora_a, rwkv_w_lora_b, rwkv_a0, rwkv_a_lora_a, rwkv_a_lora_b, rwkv_v0, rwkv_v_lora_a, rwkv_v_lora_b, rwkv_g_lora_a, rwkv_g_lora_b, rwkv_k_k, rwkv_k_a, rwkv_r_k, rwkv_ln_g, rwkv_ln_b, rwkv_w_o):
    bsz, t_len, d = x.shape
    m = bsz * t_len
    row = lambda a: a.reshape(1, -1)
    head_of_lane = jnp.arange(LANES) // HEAD_DIM
    ones = (head_of_lane[:, None] == head_of_lane[None, :]).astype(BF16)
    xf = x.reshape(m, d)
    w1, w2 = ffn_w1.astype(BF16), ffn_w2.astype(BF16)
    w_in, w_out = sgu_w_in.astype(BF16), sgu_w_out.astype(BF16)
    w_rkv, w_o = rwkv_w_rkv.astype(BF16), rwkv_w_o.astype(BF16)
    v_first = None
    for i in range(DEPTH):
        j = i // 2
        if i % 2 == 0:
            xf = _sgu(xf, row(norm_mix_g[i]), w_in, row(sgu_b_in[j]), row(sgu_ln_g[j]),
                      row(sgu_ln_b[j]), sgu_w_s[j], sgu_b_s[j][:, :, None], w_out, j)
            xf = _ffn(xf, row(norm_ffn_g[i]), w1, w2, i)
        else:
            p = dict(g=row(norm_mix_g[i]), mu=rwkv_mu[j], w_rkv=w_rkv, layer=j, w0=row(rwkv_w0[j]),
                     w_la=rwkv_w_lora_a[j].astype(BF16), w_lb=rwkv_w_lora_b[j].astype(BF16),
                     a0=row(rwkv_a0[j]), a_la=rwkv_a_lora_a[j].astype(BF16), a_lb=rwkv_a_lora_b[j].astype(BF16),
                     g_la=rwkv_g_lora_a[j].astype(BF16), g_lb=rwkv_g_lora_b[j].astype(BF16),
                     k_k=row(rwkv_k_k[j]), k_a=row(rwkv_k_a[j]), r_k=row(rwkv_r_k[j]),
                     ln_g=row(rwkv_ln_g[j]), ln_b=row(rwkv_ln_b[j]), w_o=w_o,
                     ones=ones)
            if j > 0:
                p.update(v0=row(rwkv_v0[j - 1]), v_la=rwkv_v_lora_a[j - 1].astype(BF16),
                         v_lb=rwkv_v_lora_b[j - 1].astype(BF16))
            r, w, k, v, kk, b, gate, bonus = _rwkv_in(xf, v_first if j > 0 else None, p, t_len)
            if j == 0:
                v_first = v
            sh = (bsz, t_len, d)
            y = _wkv(r.reshape(sh), w.reshape(sh), k.reshape(sh), v.reshape(sh), kk.reshape(sh), b.reshape(sh))
            xf = _rwkv_out(y.reshape(m, d), bonus, gate, xf, p, row(norm_ffn_g[i]), w1, w2, i, row(final_norm_g),
                           i == DEPTH - 1)
    return xf.reshape(bsz, t_len, d)
```

```python
import functools

import jax
import jax.numpy as jnp
from jax import lax
from jax.experimental import pallas as pl
from jax.experimental.pallas import tpu as pltpu

F32 = jnp.float32
BF16 = jnp.bfloat16

D_MODEL = 1024
DEPTH = 4
CHUNK = 64
SGU_BLOCK = 128
SGU_WIDTH = 2 * D_MODEL
SGU_GROUPS = 16
SGU_GROUP_DIM = SGU_WIDTH // SGU_GROUPS
HEAD_DIM = 64
HEADS = D_MODEL // HEAD_DIM
FFN_HIDDEN = 4 * D_MODEL
RMS_EPS = 1e-6
LN_EPS = 1e-5
GN_EPS = 64e-5

LANES = 128
WKV_CHUNK = 64
PAIR = 2 * HEAD_DIM
WKV_GROUP = 16
WKV_CHUNKS_PER_STEP = 4

TM_FFN = 512
TM_SGU = 512
TM_RWKV = 512
TM_OUT = 512
VMEM_LIMIT = 56 * 1024 * 1024


def _dot(a, b):
    return jnp.dot(a, b, preferred_element_type=F32)


def _dot_nt(a, b):
    return lax.dot_general(a, b, (((1,), (1,)), ((), ())), preferred_element_type=F32)


def _split3(x):
    hi = x.astype(BF16)
    r1 = x - hi.astype(F32)
    mid = r1.astype(BF16)
    lo = (r1 - mid.astype(F32)).astype(BF16)
    return hi, mid, lo


def _rms(x, g):
    return x * lax.rsqrt(jnp.mean(x * x, axis=-1, keepdims=True) + RMS_EPS) * g


def _gelu(x):
    return 0.5 * x * (1.0 + lax.erf(x * 0.7071067811865476))


def _sigmoid(x):
    return jax.nn.sigmoid(x)


def _const_spec(shape):
    nd = len(shape)
    return pl.BlockSpec(shape, lambda i: (0,) * nd, pipeline_mode=pl.Buffered(1))


def _layer_spec(stacked, layer):
    shape = stacked.shape[1:]
    return pl.BlockSpec((None,) + shape, lambda i: (layer,) + (0,) * len(shape), pipeline_mode=pl.Buffered(1))


def _row_spec(tm, width=D_MODEL):
    return pl.BlockSpec((tm, width), lambda i: (i, 0))


def _params():
    return pltpu.CompilerParams(dimension_semantics=("arbitrary",), vmem_limit_bytes=VMEM_LIMIT)


def _ffn_body(x, g, w1_ref, w2_ref):
    xn = _rms(x, g).astype(BF16)
    acc = x
    for c in range(FFN_HIDDEN // D_MODEL):
        sl = slice(c * D_MODEL, (c + 1) * D_MODEL)
        h = jnp.maximum(_dot(xn, w1_ref[:, sl]), 0.0)
        acc = acc + _dot((h * h).astype(BF16), w2_ref[sl, :])
    return acc


def _ffn_kernel(x_ref, g_ref, w1_ref, w2_ref, o_ref):
    o_ref[...] = _ffn_body(x_ref[...], g_ref[...], w1_ref, w2_ref)


def _ffn(x, g, w1, w2, layer):
    m = x.shape[0]
    return pl.pallas_call(
        _ffn_kernel,
        grid=(m // TM_FFN,),
        in_specs=[_row_spec(TM_FFN), _const_spec((1, D_MODEL)), _layer_spec(w1, layer), _layer_spec(w2, layer)],
        out_specs=_row_spec(TM_FFN),
        out_shape=jax.ShapeDtypeStruct((m, D_MODEL), F32),
        compiler_params=_params(),
        name="ffn",
    )(x, g, w1, w2)


def _sgu_kernel(x_ref, g_ref, win_ref, bin_ref, lng_ref, lnb_ref, ws_ref, bs_ref, wout_ref, o_ref,
                vn_ref, y_ref):
    x = x_ref[...]
    xn = _rms(x, g_ref[...]).astype(BF16)
    e = SGU_WIDTH
    zv = _gelu(_dot(xn, win_ref[:, e:]) + bin_ref[:, e:])
    mu = jnp.mean(zv, axis=-1, keepdims=True)
    var = jnp.mean(zv * zv, axis=-1, keepdims=True) - mu * mu
    vn_ref[...] = ((zv - mu) * lax.rsqrt(var + LN_EPS) * lng_ref[...] + lnb_ref[...]).astype(BF16)
    nblk = TM_SGU // SGU_BLOCK
    qi = lax.broadcasted_iota(jnp.int32, (SGU_BLOCK, SGU_BLOCK), 0) // CHUNK
    kj = lax.broadcasted_iota(jnp.int32, (SGU_BLOCK, SGU_BLOCK), 1) // CHUNK
    causal = kj <= qi
    gw = SGU_GROUP_DIM
    for gp in range(SGU_GROUPS // 2):
        cs2 = slice(2 * gp * gw, (2 * gp + 2) * gw)
        u2 = _gelu(_dot(xn, win_ref[:, cs2]) + bin_ref[:, cs2])
        for half in range(2):
            gi = 2 * gp + half
            cs = slice(gi * gw, (gi + 1) * gw)
            ws = jnp.where(causal, ws_ref[gi], 0.0).astype(BF16)
            rhs = jnp.concatenate([vn_ref[n * SGU_BLOCK:(n + 1) * SGU_BLOCK, cs] for n in range(nblk)], axis=1)
            s = _dot(ws, rhs) + bs_ref[gi]
            for n in range(nblk):
                rs = slice(n * SGU_BLOCK, (n + 1) * SGU_BLOCK)
                y_ref[rs, cs] = (u2[rs, half * gw:(half + 1) * gw] * s[:, n * gw:(n + 1) * gw]).astype(BF16)
    o_ref[...] = x + _dot(y_ref[...], wout_ref[...])


def _sgu(x, g, w_in, b_in, ln_g, ln_b, w_s, b_s, w_out, layer):
    m = x.shape[0]
    e = SGU_WIDTH
    return pl.pallas_call(
        _sgu_kernel,
        grid=(m // TM_SGU,),
        in_specs=[_row_spec(TM_SGU), _const_spec((1, D_MODEL)), _layer_spec(w_in, layer),
                  _const_spec((1, 2 * e)), _const_spec((1, e)), _const_spec((1, e)),
                  _const_spec((SGU_GROUPS, SGU_BLOCK, SGU_BLOCK)), _const_spec((SGU_GROUPS, SGU_BLOCK, 1)),
                  _layer_spec(w_out, layer)],
        out_specs=_row_spec(TM_SGU),
        out_shape=jax.ShapeDtypeStruct((m, D_MODEL), F32),
        scratch_shapes=[pltpu.VMEM((TM_SGU, e), BF16), pltpu.VMEM((TM_SGU, e), BF16)],
        compiler_params=_params(),
        name="sgu",
    )(x, g, w_in, b_in, ln_g, ln_b, w_s, b_s, w_out)


def _head_sum(x, ones_ref):
    tm, nt = x.shape[0], x.shape[1] // LANES
    stacked = jnp.concatenate([x[:, j * LANES:(j + 1) * LANES] for j in range(nt)], axis=0).astype(BF16)
    s = _dot(stacked, ones_ref[...])
    return jnp.concatenate([s[j * tm:(j + 1) * tm] for j in range(nt)], axis=1)


def _rwkv_in_kernel(*refs, tiles_per_seq, has_vmix):
    if has_vmix:
        (x_ref, xp_ref, vf_ref, g_ref, mu_ref, wrkv_ref, w0_ref, wla_ref, wlb_ref, a0_ref, ala_ref, alb_ref,
         v0_ref, vla_ref, vlb_ref, gla_ref, glb_ref, kk_ref, ka_ref, rk_ref, ones_ref,
         r_o, w_o, k_o, v_o, kk_o, b_o, g_o, bonus_o) = refs
    else:
        (x_ref, xp_ref, g_ref, mu_ref, wrkv_ref, w0_ref, wla_ref, wlb_ref, a0_ref, ala_ref, alb_ref,
         gla_ref, glb_ref, kk_ref, ka_ref, rk_ref, ones_ref,
         r_o, w_o, k_o, v_o, kk_o, b_o, g_o, bonus_o) = refs
    i = pl.program_id(0)
    gn = g_ref[...]
    h = _rms(x_ref[...], gn)
    hp = _rms(xp_ref[7:8, :], gn)
    hp = jnp.where(i % tiles_per_seq == 0, 0.0, hp)
    row = lax.broadcasted_iota(jnp.int32, h.shape, 0)
    hs = jnp.where(row == 0, hp, pltpu.roll(h, 1, 0))
    xx = hs - h

    def mix(p):
        return (h + xx * mu_ref[p:p + 1, :]).astype(BF16)

    xv = mix(2)
    tw = _dot(mix(3), wla_ref[...])
    ta = _dot(mix(4), ala_ref[...])
    tg = _dot(mix(5), gla_ref[...])
    if has_vmix:
        tv = _dot(xv, vla_ref[...])
    k = _dot(mix(1), wrkv_ref[1])
    r = _dot(mix(0), wrkv_ref[0])
    kk = k * kk_ref[...]
    ss = _head_sum(kk * kk, ones_ref)
    lw = w0_ref[...] + _dot(jnp.tanh(tw).astype(BF16), wlb_ref[...])
    a = _sigmoid(a0_ref[...] + _dot(ta.astype(BF16), alb_ref[...]))
    nrm = jnp.sqrt(ss)
    v = _dot(xv, wrkv_ref[2])
    r_o[...] = r
    k = k * (1.0 + (a - 1.0) * ka_ref[...])
    rk = _head_sum(r * k * rk_ref[...], ones_ref)
    g_o[...] = _dot(_sigmoid(tg).astype(BF16), glb_ref[...]).astype(BF16)
    if has_vmix:
        mixv = _sigmoid(v0_ref[...] + _dot(tv.astype(BF16), vlb_ref[...]))
        v = v + (vf_ref[...] - v) * mixv
    w_o[...] = -0.6065306597126334 * _sigmoid(lw)
    k_o[...] = k
    v_o[...] = v
    kk = kk / jnp.maximum(nrm, 1e-12)
    kk_o[...] = kk
    b_o[...] = kk * a
    bonus_o[...] = (rk * v).astype(BF16)


def _rwkv_in(x, v_first, p, t_len):
    m = x.shape[0]
    tm = TM_RWKV
    has_vmix = v_first is not None
    prev_spec = pl.BlockSpec((8, D_MODEL), lambda i: (jnp.maximum(i * (tm // 8) - 1, 0), 0))
    ins = [x, x] + ([v_first] if has_vmix else [])
    specs = [_row_spec(tm), prev_spec] + ([_row_spec(tm)] if has_vmix else [])

    def add(a, spec=None):
        ins.append(a)
        specs.append(spec if spec is not None else _const_spec(a.shape))

    add(p["g"]); add(p["mu"]); add(p["w_rkv"], _layer_spec(p["w_rkv"], p["layer"]))
    add(p["w0"]); add(p["w_la"]); add(p["w_lb"])
    add(p["a0"]); add(p["a_la"]); add(p["a_lb"])
    if has_vmix:
        add(p["v0"]); add(p["v_la"]); add(p["v_lb"])
    add(p["g_la"]); add(p["g_lb"]); add(p["k_k"]); add(p["k_a"]); add(p["r_k"]); add(p["ones"])
    out = jax.ShapeDtypeStruct((m, D_MODEL), F32)
    half = jax.ShapeDtypeStruct((m, D_MODEL), BF16)
    return pl.pallas_call(
        functools.partial(_rwkv_in_kernel, tiles_per_seq=t_len // tm, has_vmix=has_vmix),
        grid=(m // tm,),
        in_specs=specs,
        out_specs=[_row_spec(tm)] * 8,
        out_shape=[out] * 6 + [half] * 2,
        compiler_params=_params(),
        name="rwkv_in",
    )(*ins)


def _wkv_kernel(r_ref, w_ref, k_ref, v_ref, kk_ref, b_ref, y_ref, s_ref, *, nb, npairs):
    lc, hd, pw = WKV_CHUNK, HEAD_DIM, PAIR

    @pl.when(pl.program_id(0) == 0)
    def _():
        s_ref[...] = jnp.zeros_like(s_ref)

    lane = lax.broadcasted_iota(jnp.int32, (lc, pw), 1)
    row = lax.broadcasted_iota(jnp.int32, (lc, pw), 0)
    m0 = lane < hd
    col = jnp.where(m0, lane, lane - hd)
    strict = col < row
    incl = col <= row
    m0w = (lax.broadcasted_iota(jnp.int32, (lc, 2 * pw), 1) % pw) < hd
    tr = lax.broadcasted_iota(jnp.int32, (lc, 3 * lc), 0)
    tc = lax.broadcasted_iota(jnp.int32, (lc, 3 * lc), 1) % lc
    tri3 = jnp.where(tc <= tr, 1.0, 0.0).astype(BF16)

    def bdiag(x):
        z = jnp.zeros_like(x)
        return jnp.concatenate([jnp.where(m0, x, z), jnp.where(m0, z, x)], axis=0)

    for c in range(WKV_CHUNKS_PER_STEP):
        _wkv_chunk(slice(c * lc, (c + 1) * lc), r_ref, w_ref, k_ref, v_ref, kk_ref, b_ref, y_ref, s_ref,
                   nb, npairs, m0, strict, incl, m0w, tri3, bdiag)


def _wkv_chunk(ts, r_ref, w_ref, k_ref, v_ref, kk_ref, b_ref, y_ref, s_ref, nb, npairs, m0, strict, incl, m0w,
               tri3, bdiag):
    lc, hd, pw = WKV_CHUNK, HEAD_DIM, PAIR
    chains = []
    for bi in range(nb):
        w = w_ref[bi, ts, :]
        cum = _dot(tri3, jnp.concatenate(_split3(w), axis=0))
        tot = cum[lc - 1:lc, :]
        e_incl = jnp.exp(cum)
        e_excl = jnp.exp(cum - w)
        e_inv = jnp.exp(-cum)
        e_rem = jnp.exp(tot - cum)
        k = k_ref[bi, ts, :]
        b = b_ref[bi, ts, :]
        chains.extend((bi, p,
                       (-(kk_ref[bi, ts, :] * e_excl)).astype(BF16), (r_ref[bi, ts, :] * e_incl).astype(BF16),
                       (b * e_inv).astype(BF16), (k * e_inv).astype(BF16),
                       b * e_rem, k * e_rem,
                       v_ref[bi, ts, :].astype(BF16), jnp.exp(tot)) for p in range(npairs))

    for g0 in range(0, len(chains), WKV_GROUP):
        grp = chains[g0:g0 + WKV_GROUP]
        n = len(grp)
        sls = [slice(c[1] * pw, (c[1] + 1) * pw) for c in grp]
        at, rt, bt, kt, bh, kh, vp, pt = ([c[j][:, sl] for c, sl in zip(grp, sls)] for j in range(2, 10))
        sidx = [c[0] * npairs + c[1] for c in grp]
        ar = [jnp.concatenate([at[i], rt[i]], axis=0) for i in range(n)]
        g = [_dot_nt(ar[i], jnp.concatenate([bdiag(bt[i]), bdiag(kt[i])], axis=0)) for i in range(n)]
        nmat = [jnp.where(strict, g[i][:lc, :pw], 0.0) for i in range(n)]
        aak = [jnp.where(strict, g[i][:lc, pw:], 0.0).astype(BF16) for i in range(n)]
        arbk = [jnp.concatenate([jnp.where(incl, g[i][lc:, :pw], 0.0), jnp.where(incl, g[i][lc:, pw:], 0.0)],
                                axis=1).astype(BF16) for i in range(n)]
        s = [s_ref[sidx[i]] for i in range(n)]
        vbd = [bdiag(vp[i]) for i in range(n)]
        sbd = [bdiag(s[i].astype(BF16)) for i in range(n)]
        wmat = [_dot(jnp.concatenate([at[i], aak[i]], axis=1), jnp.concatenate([sbd[i], vbd[i]], axis=0))
                for i in range(n)]
        rs = [_dot(rt[i], sbd[i]) for i in range(n)]
        for it in range(6):
            if it < 5:
                o = []
                for i in range(n):
                    pm = jnp.concatenate([nmat[i], wmat[i]], axis=1).astype(BF16)
                    z = jnp.zeros_like(pm)
                    rhs = jnp.concatenate([jnp.where(m0w, pm, z), jnp.where(m0w, z, pm)], axis=0)
                    o.append(_dot(nmat[i].astype(BF16), rhs))
                wmat = [wmat[i] + o[i][:, pw:] for i in range(n)]
                nmat = [o[i][:, :pw] for i in range(n)]
            else:
                o = [_dot(nmat[i].astype(BF16), bdiag(wmat[i].astype(BF16))) for i in range(n)]
                wmat = [wmat[i] + o[i] for i in range(n)]
        u16 = [wmat[i].astype(BF16) for i in range(n)]
        y = [rs[i] + _dot(arbk[i], jnp.concatenate([bdiag(u16[i]), vbd[i]], axis=0)) for i in range(n)]
        upd = []
        pcol = []
        for i in range(n):
            ubd = bdiag(u16[i])
            tk = jnp.transpose(jnp.concatenate([bh[i], kh[i]], axis=0)).astype(BF16)
            lhs = jnp.concatenate([tk[:hd], tk[hd:]], axis=1)
            upd.append(_dot(lhs, jnp.concatenate([ubd[:lc], vbd[i][:lc], ubd[lc:], vbd[i][lc:]], axis=0)))
            tp = jnp.transpose(jnp.broadcast_to(pt[i], (pw, pw)))
            pcol.append(jnp.where(m0, tp[:hd], tp[hd:]))
        for i in range(n):
            y_ref[grp[i][0], ts, sls[i]] = y[i]
            s_ref[sidx[i]] = s[i] * pcol[i] + upd[i]


def _wkv(r, w, k, v, kk, b):
    bsz, t_len, d = r.shape
    npairs = d // PAIR
    t_blk = WKV_CHUNKS_PER_STEP * WKV_CHUNK
    spec = pl.BlockSpec((bsz, t_blk, d), lambda t: (0, t, 0))
    return pl.pallas_call(
        functools.partial(_wkv_kernel, nb=bsz, npairs=npairs),
        grid=(t_len // t_blk,),
        in_specs=[spec] * 6,
        out_specs=spec,
        out_shape=jax.ShapeDtypeStruct((bsz, t_len, d), F32),
        scratch_shapes=[pltpu.VMEM((bsz * npairs, HEAD_DIM, PAIR), F32)],
        compiler_params=_params(),
        name="wkv7",
    )(r, w, k, v, kk, b)


def _rwkv_out_kernel(y_ref, bonus_ref, gate_ref, x_ref, lng_ref, lnb_ref, wo_ref, ones_ref,
                     g_ref, w1_ref, w2_ref, fg_ref, o_ref, *, final_norm):
    hr = y_ref.shape[0] // 2
    rsl = [slice(i * hr, (i + 1) * hr) for i in range(2)]
    y = [y_ref[r, :] for r in rsl]
    mu = [_head_sum(v, ones_ref) * (1.0 / HEAD_DIM) for v in y]
    yc = [a - b for a, b in zip(y, mu)]
    var = [_head_sum(v * v, ones_ref) * (1.0 / HEAD_DIM) for v in yc]
    yn = [a * lax.rsqrt(b + GN_EPS) * lng_ref[...] + lnb_ref[...] for a, b in zip(yc, var)]
    out = [((a + bonus_ref[r, :]) * gate_ref[r, :]).astype(BF16) for a, r in zip(yn, rsl)]
    x = jnp.concatenate([x_ref[r, :] + _dot(a, wo_ref[...]) for a, r in zip(out, rsl)], axis=0)
    res = _ffn_body(x, g_ref[...], w1_ref, w2_ref)
    if final_norm:
        res = _rms(res, fg_ref[...])
    o_ref[...] = res


def _rwkv_out(y, bonus, gate, x, p, g, w1, w2, ffn_layer, fg, final_norm):
    m = x.shape[0]
    tm = TM_OUT
    vec = _const_spec((1, D_MODEL))
    return pl.pallas_call(
        functools.partial(_rwkv_out_kernel, final_norm=final_norm),
        grid=(m // tm,),
        in_specs=[_row_spec(tm)] * 4 + [vec, vec, _layer_spec(p["w_o"], p["layer"]),
                                        _const_spec(p["ones"].shape), vec,
                                        _layer_spec(w1, ffn_layer), _layer_spec(w2, ffn_layer), vec],
        out_specs=_row_spec(tm),
        out_shape=jax.ShapeDtypeStruct((m, D_MODEL), F32),
        compiler_params=_params(),
        name="rwkv_out_ffn",
    )(y, bonus, gate, x, p["ln_g"], p["ln_b"], p["w_o"], p["ones"], g, w1, w2, fg)


def kernel(x, norm_mix_g, norm_ffn_g, final_norm_g, ffn_w1, ffn_w2, sgu_w_in, sgu_b_in, sgu_ln_g, sgu_ln_b, sgu_w_s, sgu_b_s, sgu_w_out, rwkv_mu, rwkv_w_rkv, rwkv_w0, rwkv_w_lora_a, rwkv_w_lora_b, rwkv_a0, rwkv_a_lora_a, rwkv_a_lora_b, rwkv_v0, rwkv_v_lora_a, rwkv_v_lora_b, rwkv_g_lora_a, rwkv_g_lora_b, rwkv_k_k, rwkv_k_a, rwkv_r_k, rwkv_ln_g, rwkv_ln_b, rwkv_w_o):
    bsz, t_len, d = x.shape
    m = bsz * t_len
    row = lambda a: a.reshape(1, -1)
    head_of_lane = jnp.arange(LANES) // HEAD_DIM
    ones = (head_of_lane[:, None] == head_of_lane[None, :]).astype(BF16)
    xf = x.reshape(m, d)
    w1, w2 = ffn_w1.astype(BF16), ffn_w2.astype(BF16)
    w_in, w_out = sgu_w_in.astype(BF16), sgu_w_out.astype(BF16)
    w_rkv, w_o = rwkv_w_rkv.astype(BF16), rwkv_w_o.astype(BF16)
    v_first = None
    for i in range(DEPTH):
        j = i // 2
        if i % 2 == 0:
            xf = _sgu(xf, row(norm_mix_g[i]), w_in, row(sgu_b_in[j]), row(sgu_ln_g[j]),
                      row(sgu_ln_b[j]), sgu_w_s[j], sgu_b_s[j][:, :, None], w_out, j)
            xf = _ffn(xf, row(norm_ffn_g[i]), w1, w2, i)
        else:
            p = dict(g=row(norm_mix_g[i]), mu=rwkv_mu[j], w_rkv=w_rkv, layer=j, w0=row(rwkv_w0[j]),
                     w_la=rwkv_w_lora_a[j].astype(BF16), w_lb=rwkv_w_lora_b[j].astype(BF16),
                     a0=row(rwkv_a0[j]), a_la=rwkv_a_lora_a[j].astype(BF16), a_lb=rwkv_a_lora_b[j].astype(BF16),
                     g_la=rwkv_g_lora_a[j].astype(BF16), g_lb=rwkv_g_lora_b[j].astype(BF16),
                     k_k=row(rwkv_k_k[j]), k_a=row(rwkv_k_a[j]), r_k=row(rwkv_r_k[j]),
                     ln_g=row(rwkv_ln_g[j]), ln_b=row(rwkv_ln_b[j]), w_o=w_o,
                     ones=ones)
            if j > 0:
                p.update(v0=row(rwkv_v0[j - 1]), v_la=rwkv_v_lora_a[j - 1].astype(BF16),
                         v_lb=rwkv_v_lora_b[j - 1].astype(BF16))
            r, w, k, v, kk, b, gate, bonus = _rwkv_in(xf, v_first if j > 0 else None, p, t_len)
            if j == 0:
                v_first = v
            sh = (bsz, t_len, d)
            y = _wkv(r.reshape(sh), w.reshape(sh), k.reshape(sh), v.reshape(sh), kk.reshape(sh), b.reshape(sh))
            xf = _rwkv_out(y.reshape(m, d), bonus, gate, xf, p, row(norm_ffn_g[i]), w1, w2, i, row(final_norm_g),
                           i == DEPTH - 1)
    return xf.reshape(bsz, t_len, d)
```

```python
import functools

import jax
import jax.numpy as jnp
from jax import lax
from jax.experimental import pallas as pl
from jax.experimental.pallas import tpu as pltpu

F32 = jnp.float32
BF16 = jnp.bfloat16

D_MODEL = 1024
DEPTH = 4
CHUNK = 64
SGU_BLOCK = 128
SGU_WIDTH = 2 * D_MODEL
SGU_GROUPS = 16
SGU_GROUP_DIM = SGU_WIDTH // SGU_GROUPS
HEAD_DIM = 64
HEADS = D_MODEL // HEAD_DIM
FFN_HIDDEN = 4 * D_MODEL
RMS_EPS = 1e-6
LN_EPS = 1e-5
GN_EPS = 64e-5

LANES = 128
WKV_CHUNK = 64
PAIR = 2 * HEAD_DIM
WKV_CHUNKS_PER_STEP = 4

TM_FFN = 1024
TM_SGU = 512
TM_RWKV = 512
TM_OUT = 512
VMEM_LIMIT = 56 * 1024 * 1024


def _dot(a, b):
    return jnp.dot(a, b, preferred_element_type=F32)


def _dot_nt(a, b):
    return lax.dot_general(a, b, (((1,), (1,)), ((), ())), preferred_element_type=F32)


def _split3(x):
    hi = x.astype(BF16)
    r1 = x - hi.astype(F32)
    mid = r1.astype(BF16)
    lo = (r1 - mid.astype(F32)).astype(BF16)
    return hi, mid, lo


def _rms(x, g):
    return x * lax.rsqrt(jnp.mean(x * x, axis=-1, keepdims=True) + RMS_EPS) * g


def _gelu(x):
    return 0.5 * x * (1.0 + lax.erf(x * 0.7071067811865476))


def _sigmoid(x):
    return jax.nn.sigmoid(x)


def _const_spec(shape):
    nd = len(shape)
    return pl.BlockSpec(shape, lambda i: (0,) * nd, pipeline_mode=pl.Buffered(1))


def _layer_spec(stacked, layer):
    shape = stacked.shape[1:]
    return pl.BlockSpec((None,) + shape, lambda i: (layer,) + (0,) * len(shape), pipeline_mode=pl.Buffered(1))


def _row_spec(tm, width=D_MODEL):
    return pl.BlockSpec((tm, width), lambda i: (i, 0))


def _params():
    return pltpu.CompilerParams(dimension_semantics=("arbitrary",), vmem_limit_bytes=VMEM_LIMIT)


def _ffn_body(x, g, w1_ref, w2_ref):
    xn = _rms(x, g).astype(BF16)
    acc = x
    for c in range(FFN_HIDDEN // D_MODEL):
        sl = slice(c * D_MODEL, (c + 1) * D_MODEL)
        h = jnp.maximum(_dot(xn, w1_ref[:, sl]), 0.0)
        acc = acc + _dot((h * h).astype(BF16), w2_ref[sl, :])
    return acc


def _ffn_kernel(x_ref, g_ref, w1_ref, w2_ref, o_ref):
    o_ref[...] = _ffn_body(x_ref[...], g_ref[...], w1_ref, w2_ref)


def _ffn(x, g, w1, w2, layer):
    m = x.shape[0]
    return pl.pallas_call(
        _ffn_kernel,
        grid=(m // TM_FFN,),
        in_specs=[_row_spec(TM_FFN), _const_spec((1, D_MODEL)), _layer_spec(w1, layer), _layer_spec(w2, layer)],
        out_specs=_row_spec(TM_FFN),
        out_shape=jax.ShapeDtypeStruct((m, D_MODEL), F32),
        compiler_params=_params(),
        name="ffn",
    )(x, g, w1, w2)


def _sgu_kernel(x_ref, g_ref, win_ref, bin_ref, lng_ref, lnb_ref, ws_ref, bs_ref, wout_ref, o_ref,
                vn_ref, y_ref):
    x = x_ref[...]
    xn = _rms(x, g_ref[...]).astype(BF16)
    e = SGU_WIDTH
    zv = _gelu(_dot(xn, win_ref[:, e:]) + bin_ref[:, e:])
    mu = jnp.mean(zv, axis=-1, keepdims=True)
    var = jnp.mean(zv * zv, axis=-1, keepdims=True) - mu * mu
    vn_ref[...] = ((zv - mu) * lax.rsqrt(var + LN_EPS) * lng_ref[...] + lnb_ref[...]).astype(BF16)
    nblk = TM_SGU // SGU_BLOCK
    qi = lax.broadcasted_iota(jnp.int32, (SGU_BLOCK, SGU_BLOCK), 0) // CHUNK
    kj = lax.broadcasted_iota(jnp.int32, (SGU_BLOCK, SGU_BLOCK), 1) // CHUNK
    causal = kj <= qi
    gw = SGU_GROUP_DIM
    for gp in range(SGU_GROUPS // 2):
        cs2 = slice(2 * gp * gw, (2 * gp + 2) * gw)
        u2 = _gelu(_dot(xn, win_ref[:, cs2]) + bin_ref[:, cs2])
        for half in range(2):
            gi = 2 * gp + half
            cs = slice(gi * gw, (gi + 1) * gw)
            ws = jnp.where(causal, ws_ref[gi], 0.0).astype(BF16)
            rhs = jnp.concatenate([vn_ref[n * SGU_BLOCK:(n + 1) * SGU_BLOCK, cs] for n in range(nblk)], axis=1)
            s = _dot(ws, rhs) + bs_ref[gi]
            for n in range(nblk):
                rs = slice(n * SGU_BLOCK, (n + 1) * SGU_BLOCK)
                y_ref[rs, cs] = (u2[rs, half * gw:(half + 1) * gw] * s[:, n * gw:(n + 1) * gw]).astype(BF16)
    o_ref[...] = x + _dot(y_ref[...], wout_ref[...])


def _sgu(x, g, w_in, b_in, ln_g, ln_b, w_s, b_s, w_out, layer):
    m = x.shape[0]
    e = SGU_WIDTH
    return pl.pallas_call(
        _sgu_kernel,
        grid=(m // TM_SGU,),
        in_specs=[_row_spec(TM_SGU), _const_spec((1, D_MODEL)), _layer_spec(w_in, layer),
                  _const_spec((1, 2 * e)), _const_spec((1, e)), _const_spec((1, e)),
                  _const_spec((SGU_GROUPS, SGU_BLOCK, SGU_BLOCK)), _const_spec((SGU_GROUPS, SGU_BLOCK, 1)),
                  _layer_spec(w_out, layer)],
        out_specs=_row_spec(TM_SGU),
        out_shape=jax.ShapeDtypeStruct((m, D_MODEL), F32),
        scratch_shapes=[pltpu.VMEM((TM_SGU, e), BF16), pltpu.VMEM((TM_SGU, e), BF16)],
        compiler_params=_params(),
        name="sgu",
    )(x, g, w_in, b_in, ln_g, ln_b, w_s, b_s, w_out)


def _head_sum(x, ones_ref):
    tm, nt = x.shape[0], x.shape[1] // LANES
    stacked = jnp.concatenate([x[:, j * LANES:(j + 1) * LANES] for j in range(nt)], axis=0).astype(BF16)
    s = _dot(stacked, ones_ref[...])
    return jnp.concatenate([s[j * tm:(j + 1) * tm] for j in range(nt)], axis=1)


def _rwkv_in_kernel(*refs, tiles_per_seq, has_vmix):
    if has_vmix:
        (x_ref, xp_ref, vf_ref, g_ref, mu_ref, wrkv_ref, w0_ref, wla_ref, wlb_ref, a0_ref, ala_ref, alb_ref,
         v0_ref, vla_ref, vlb_ref, gla_ref, glb_ref, kk_ref, ka_ref, rk_ref, ones_ref,
         r_o, w_o, k_o, v_o, kk_o, b_o, g_o, bonus_o) = refs
    else:
        (x_ref, xp_ref, g_ref, mu_ref, wrkv_ref, w0_ref, wla_ref, wlb_ref, a0_ref, ala_ref, alb_ref,
         gla_ref, glb_ref, kk_ref, ka_ref, rk_ref, ones_ref,
         r_o, w_o, k_o, v_o, kk_o, b_o, g_o, bonus_o) = refs
    i = pl.program_id(0)
    gn = g_ref[...]
    h = _rms(x_ref[...], gn)
    hp = _rms(xp_ref[7:8, :], gn)
    hp = jnp.where(i % tiles_per_seq == 0, 0.0, hp)
    row = lax.broadcasted_iota(jnp.int32, h.shape, 0)
    hs = jnp.where(row == 0, hp, pltpu.roll(h, 1, 0))
    xx = hs - h

    def mix(p):
        return (h + xx * mu_ref[p:p + 1, :]).astype(BF16)

    xv = mix(2)
    tw = _dot(mix(3), wla_ref[...])
    ta = _dot(mix(4), ala_ref[...])
    tg = _dot(mix(5), gla_ref[...])
    if has_vmix:
        tv = _dot(xv, vla_ref[...])
    k = _dot(mix(1), wrkv_ref[1])
    r = _dot(mix(0), wrkv_ref[0])
    kk = k * kk_ref[...]
    ss = _head_sum(kk * kk, ones_ref)
    lw = w0_ref[...] + _dot(jnp.tanh(tw).astype(BF16), wlb_ref[...])
    a = _sigmoid(a0_ref[...] + _dot(ta.astype(BF16), alb_ref[...]))
    nrm = jnp.sqrt(ss)
    v = _dot(xv, wrkv_ref[2])
    r_o[...] = r
    k = k * (1.0 + (a - 1.0) * ka_ref[...])
    rk = _head_sum(r * k * rk_ref[...], ones_ref)
    g_o[...] = _dot(_sigmoid(tg).astype(BF16), glb_ref[...]).astype(BF16)
    if has_vmix:
        mixv = _sigmoid(v0_ref[...] + _dot(tv.astype(BF16), vlb_ref[...]))
        v = v + (vf_ref[...] - v) * mixv
    w_o[...] = -0.6065306597126334 * _sigmoid(lw)
    k_o[...] = k
    v_o[...] = v
    kk = kk / jnp.maximum(nrm, 1e-12)
    kk_o[...] = kk
    b_o[...] = kk * a
    bonus_o[...] = (rk * v).astype(BF16)


def _rwkv_in(x, v_first, p, t_len):
    m = x.shape[0]
    tm = TM_RWKV
    has_vmix = v_first is not None
    prev_spec = pl.BlockSpec((8, D_MODEL), lambda i: (jnp.maximum(i * (tm // 8) - 1, 0), 0))
    ins = [x, x] + ([v_first] if has_vmix else [])
    specs = [_row_spec(tm), prev_spec] + ([_row_spec(tm)] if has_vmix else [])

    def add(a, spec=None):
        ins.append(a)
        specs.append(spec if spec is not None else _const_spec(a.shape))

    add(p["g"]); add(p["mu"]); add(p["w_rkv"], _layer_spec(p["w_rkv"], p["layer"]))
    add(p["w0"]); add(p["w_la"]); add(p["w_lb"])
    add(p["a0"]); add(p["a_la"]); add(p["a_lb"])
    if has_vmix:
        add(p["v0"]); add(p["v_la"]); add(p["v_lb"])
    add(p["g_la"]); add(p["g_lb"]); add(p["k_k"]); add(p["k_a"]); add(p["r_k"]); add(p["ones"])
    out = jax.ShapeDtypeStruct((m, D_MODEL), F32)
    half = jax.ShapeDtypeStruct((m, D_MODEL), BF16)
    return pl.pallas_call(
        functools.partial(_rwkv_in_kernel, tiles_per_seq=t_len // tm, has_vmix=has_vmix),
        grid=(m // tm,),
        in_specs=specs,
        out_specs=[_row_spec(tm)] * 8,
        out_shape=[out] * 6 + [half] * 2,
        compiler_params=_params(),
        name="rwkv_in",
    )(*ins)


def _wkv_kernel(r_ref, w_ref, k_ref, v_ref, kk_ref, b_ref, y_ref, s_ref, *, nb, npairs):
    lc, hd, pw = WKV_CHUNK, HEAD_DIM, PAIR

    @pl.when(pl.program_id(0) == 0)
    def _():
        s_ref[...] = jnp.zeros_like(s_ref)

    lane = lax.broadcasted_iota(jnp.int32, (lc, pw), 1)
    row = lax.broadcasted_iota(jnp.int32, (lc, pw), 0)
    m0 = lane < hd
    col = jnp.where(m0, lane, lane - hd)
    strict = col < row
    incl = col <= row
    m0w = (lax.broadcasted_iota(jnp.int32, (lc, 2 * pw), 1) % pw) < hd
    tr = lax.broadcasted_iota(jnp.int32, (lc, 3 * lc), 0)
    tc = lax.broadcasted_iota(jnp.int32, (lc, 3 * lc), 1) % lc
    tri3 = jnp.where(tc <= tr, 1.0, 0.0).astype(BF16)

    def bdiag(x):
        z = jnp.zeros_like(x)
        return jnp.concatenate([jnp.where(m0, x, z), jnp.where(m0, z, x)], axis=0)

    tss = [slice(c * lc, (c + 1) * lc) for c in range(WKV_CHUNKS_PER_STEP)]

    def front(c):
        return _wkv_front(tss[c], r_ref, w_ref, k_ref, v_ref, kk_ref, b_ref, nb, npairs, strict, incl, tri3, bdiag)

    cur = front(0)
    for c in range(len(tss)):
        nxt = functools.partial(front, c + 1) if c + 1 < len(tss) else None
        cur = _wkv_state(tss[c], cur, y_ref, s_ref, npairs, m0, m0w, bdiag, nxt)


def _wkv_front(ts, r_ref, w_ref, k_ref, v_ref, kk_ref, b_ref, nb, npairs, strict, incl, tri3, bdiag):
    lc, pw = WKV_CHUNK, PAIR
    chains = []
    for bi in range(nb):
        w = w_ref[bi, ts, :]
        cum = _dot(tri3, jnp.concatenate(_split3(w), axis=0))
        tot = cum[lc - 1:lc, :]
        e_incl = jnp.exp(cum)
        e_excl = jnp.exp(cum - w)
        e_inv = jnp.exp(-cum)
        e_rem = jnp.exp(tot - cum)
        k = k_ref[bi, ts, :]
        b = b_ref[bi, ts, :]
        chains.extend((bi, p,
                       (-(kk_ref[bi, ts, :] * e_excl)).astype(BF16), (r_ref[bi, ts, :] * e_incl).astype(BF16),
                       (b * e_inv).astype(BF16), (k * e_inv).astype(BF16),
                       b * e_rem, k * e_rem,
                       v_ref[bi, ts, :].astype(BF16), jnp.exp(tot)) for p in range(npairs))

    n = len(chains)
    sls = [slice(c[1] * pw, (c[1] + 1) * pw) for c in chains]
    at, rt, bt, kt, bh, kh, vp, pt = ([c[j][:, sl] for c, sl in zip(chains, sls)] for j in range(2, 10))
    ar = [jnp.concatenate([at[i], rt[i]], axis=0) for i in range(n)]
    g = [_dot_nt(ar[i], jnp.concatenate([bdiag(bt[i]), bdiag(kt[i])], axis=0)) for i in range(n)]
    nmat = [jnp.where(strict, g[i][:lc, :pw], 0.0) for i in range(n)]
    aak = [jnp.where(strict, g[i][:lc, pw:], 0.0).astype(BF16) for i in range(n)]
    arbk = [jnp.concatenate([jnp.where(incl, g[i][lc:, :pw], 0.0), jnp.where(incl, g[i][lc:, pw:], 0.0)],
                            axis=1).astype(BF16) for i in range(n)]
    vbd = [bdiag(vp[i]) for i in range(n)]
    return dict(chains=chains, sls=sls, at=at, rt=rt, bh=bh, kh=kh, pt=pt, nmat=nmat, aak=aak, arbk=arbk, vbd=vbd)


def _wkv_state(ts, fr, y_ref, s_ref, npairs, m0, m0w, bdiag, trace_next):
    lc, hd, pw = WKV_CHUNK, HEAD_DIM, PAIR
    chains, sls, at, rt, bh, kh, pt, nmat, aak, arbk, vbd = (fr[q] for q in (
        "chains", "sls", "at", "rt", "bh", "kh", "pt", "nmat", "aak", "arbk", "vbd"))
    n = len(chains)
    sidx = [c[0] * npairs + c[1] for c in chains]
    s = [s_ref[sidx[i]] for i in range(n)]
    sbd = [bdiag(s[i].astype(BF16)) for i in range(n)]
    wmat = [_dot(jnp.concatenate([at[i], aak[i]], axis=1), jnp.concatenate([sbd[i], vbd[i]], axis=0))
            for i in range(n)]
    rs = [_dot(rt[i], sbd[i]) for i in range(n)]
    nxt = trace_next() if trace_next is not None else None
    for it in range(6):
        if it < 5:
            o = []
            for i in range(n):
                pm = jnp.concatenate([nmat[i], wmat[i]], axis=1).astype(BF16)
                z = jnp.zeros_like(pm)
                rhs = jnp.concatenate([jnp.where(m0w, pm, z), jnp.where(m0w, z, pm)], axis=0)
                o.append(_dot(nmat[i].astype(BF16), rhs))
            wmat = [wmat[i] + o[i][:, pw:] for i in range(n)]
            nmat = [o[i][:, :pw] for i in range(n)]
        else:
            o = [_dot(nmat[i].astype(BF16), bdiag(wmat[i].astype(BF16))) for i in range(n)]
            wmat = [wmat[i] + o[i] for i in range(n)]
    u16 = [wmat[i].astype(BF16) for i in range(n)]
    y = [rs[i] + _dot(arbk[i], jnp.concatenate([bdiag(u16[i]), vbd[i]], axis=0)) for i in range(n)]
    upd = []
    pcol = []
    for i in range(n):
        ubd = bdiag(u16[i])
        tk = jnp.transpose(jnp.concatenate([bh[i], kh[i]], axis=0)).astype(BF16)
        lhs = jnp.concatenate([tk[:hd], tk[hd:]], axis=1)
        upd.append(_dot(lhs, jnp.concatenate([ubd[:lc], vbd[i][:lc], ubd[lc:], vbd[i][lc:]], axis=0)))
        tp = jnp.transpose(jnp.broadcast_to(pt[i], (pw, pw)))
        pcol.append(jnp.where(m0, tp[:hd], tp[hd:]))
    for i in range(n):
        y_ref[chains[i][0], ts, sls[i]] = y[i]
        s_ref[sidx[i]] = s[i] * pcol[i] + upd[i]
    return nxt


def _wkv(r, w, k, v, kk, b):
    bsz, t_len, d = r.shape
    npairs = d // PAIR
    t_blk = WKV_CHUNKS_PER_STEP * WKV_CHUNK
    spec = pl.BlockSpec((bsz, t_blk, d), lambda t: (0, t, 0))
    return pl.pallas_call(
        functools.partial(_wkv_kernel, nb=bsz, npairs=npairs),
        grid=(t_len // t_blk,),
        in_specs=[spec] * 6,
        out_specs=spec,
        out_shape=jax.ShapeDtypeStruct((bsz, t_len, d), F32),
        scratch_shapes=[pltpu.VMEM((bsz * npairs, HEAD_DIM, PAIR), F32)],
        compiler_params=_params(),
        name="wkv7",
    )(r, w, k, v, kk, b)


def _rwkv_out_kernel(y_ref, bonus_ref, gate_ref, x_ref, lng_ref, lnb_ref, wo_ref, ones_ref,
                     g_ref, w1_ref, w2_ref, fg_ref, o_ref, *, final_norm):
    hr = y_ref.shape[0] // 2
    rsl = [slice(i * hr, (i + 1) * hr) for i in range(2)]
    y = [y_ref[r, :] for r in rsl]
    mu = [_head_sum(v, ones_ref) * (1.0 / HEAD_DIM) for v in y]
    yc = [a - b for a, b in zip(y, mu)]
    var = [_head_sum(v * v, ones_ref) * (1.0 / HEAD_DIM) for v in yc]
    yn = [a * lax.rsqrt(b + GN_EPS) * lng_ref[...] + lnb_ref[...] for a, b in zip(yc, var)]
    out = [((a + bonus_ref[r, :]) * gate_ref[r, :]).astype(BF16) for a, r in zip(yn, rsl)]
    x = jnp.concatenate([x_ref[r, :] + _dot(a, wo_ref[...]) for a, r in zip(out, rsl)], axis=0)
    res = _ffn_body(x, g_ref[...], w1_ref, w2_ref)
    if final_norm:
        res = _rms(res, fg_ref[...])
    o_ref[...] = res


def _rwkv_out(y, bonus, gate, x, p, g, w1, w2, ffn_layer, fg, final_norm):
    m = x.shape[0]
    tm = TM_OUT
    vec = _const_spec((1, D_MODEL))
    return pl.pallas_call(
        functools.partial(_rwkv_out_kernel, final_norm=final_norm),
        grid=(m // tm,),
        in_specs=[_row_spec(tm)] * 4 + [vec, vec, _layer_spec(p["w_o"], p["layer"]),
                                        _const_spec(p["ones"].shape), vec,
                                        _layer_spec(w1, ffn_layer), _layer_spec(w2, ffn_layer), vec],
        out_specs=_row_spec(tm),
        out_shape=jax.ShapeDtypeStruct((m, D_MODEL), F32),
        compiler_params=_params(),
        name="rwkv_out_ffn",
    )(y, bonus, gate, x, p["ln_g"], p["ln_b"], p["w_o"], p["ones"], g, w1, w2, fg)


def kernel(x, norm_mix_g, norm_ffn_g, final_norm_g, ffn_w1, ffn_w2, sgu_w_in, sgu_b_in, sgu_ln_g, sgu_ln_b, sgu_w_s, sgu_b_s, sgu_w_out, rwkv_mu, rwkv_w_rkv, rwkv_w0, rwkv_w_lora_a, rwkv_w_lora_b, rwkv_a0, rwkv_a_lora_a, rwkv_a_lora_b, rwkv_v0, rwkv_v_lora_a, rwkv_v_lora_b, rwkv_g_lora_a, rwkv_g_lora_b, rwkv_k_k, rwkv_k_a, rwkv_r_k, rwkv_ln_g, rwkv_ln_b, rwkv_w_o):
    bsz, t_len, d = x.shape
    m = bsz * t_len
    row = lambda a: a.reshape(1, -1)
    head_of_lane = jnp.arange(LANES) // HEAD_DIM
    ones = (head_of_lane[:, None] == head_of_lane[None, :]).astype(BF16)
    xf = x.reshape(m, d)
    w1, w2 = ffn_w1.astype(BF16), ffn_w2.astype(BF16)
    w_in, w_out = sgu_w_in.astype(BF16), sgu_w_out.astype(BF16)
    w_rkv, w_o = rwkv_w_rkv.astype(BF16), rwkv_w_o.astype(BF16)
    v_first = None
    for i in range(DEPTH):
        j = i // 2
        if i % 2 == 0:
            xf = _sgu(xf, row(norm_mix_g[i]), w_in, row(sgu_b_in[j]), row(sgu_ln_g[j]),
                      row(sgu_ln_b[j]), sgu_w_s[j], sgu_b_s[j][:, :, None], w_out, j)
            xf = _ffn(xf, row(norm_ffn_g[i]), w1, w2, i)
        else:
            p = dict(g=row(norm_mix_g[i]), mu=rwkv_mu[j], w_rkv=w_rkv, layer=j, w0=row(rwkv_w0[j]),
                     w_la=rwkv_w_lora_a[j].astype(BF16), w_lb=rwkv_w_lora_b[j].astype(BF16),
                     a0=row(rwkv_a0[j]), a_la=rwkv_a_lora_a[j].astype(BF16), a_lb=rwkv_a_lora_b[j].astype(BF16),
                     g_la=rwkv_g_lora_a[j].astype(BF16), g_lb=rwkv_g_lora_b[j].astype(BF16),
                     k_k=row(rwkv_k_k[j]), k_a=row(rwkv_k_a[j]), r_k=row(rwkv_r_k[j]),
                     ln_g=row(rwkv_ln_g[j]), ln_b=row(rwkv_ln_b[j]), w_o=w_o,
                     ones=ones)
            if j > 0:
                p.update(v0=row(rwkv_v0[j - 1]), v_la=rwkv_v_lora_a[j - 1].astype(BF16),
                         v_lb=rwkv_v_lora_b[j - 1].astype(BF16))
            r, w, k, v, kk, b, gate, bonus = _rwkv_in(xf, v_first if j > 0 else None, p, t_len)
            if j == 0:
                v_first = v
            sh = (bsz, t_len, d)
            y = _wkv(r.reshape(sh), w.reshape(sh), k.reshape(sh), v.reshape(sh), kk.reshape(sh), b.reshape(sh))
            xf = _rwkv_out(y.reshape(m, d), bonus, gate, xf, p, row(norm_ffn_g[i]), w1, w2, i, row(final_norm_g),
                           i == DEPTH - 1)
    return xf.reshape(bsz, t_len, d)
```

```python
import functools

import jax
import jax.numpy as jnp
from jax import lax
from jax.experimental import pallas as pl
from jax.experimental.pallas import tpu as pltpu

F32 = jnp.float32
BF16 = jnp.bfloat16

D_MODEL = 1024
DEPTH = 4
CHUNK = 64
SGU_BLOCK = 128
SGU_WIDTH = 2 * D_MODEL
SGU_GROUPS = 16
SGU_GROUP_DIM = SGU_WIDTH // SGU_GROUPS
HEAD_DIM = 64
HEADS = D_MODEL // HEAD_DIM
FFN_HIDDEN = 4 * D_MODEL
RMS_EPS = 1e-6
LN_EPS = 1e-5
GN_EPS = 64e-5

LANES = 128
WKV_CHUNK = 64
PAIR = 2 * HEAD_DIM
WKV_CHUNKS_PER_STEP = 4

TM_FFN = 1024
TM_SGU = 512
TM_RWKV = 512
TM_OUT = 512
VMEM_LIMIT = 56 * 1024 * 1024


def _dot(a, b):
    return jnp.dot(a, b, preferred_element_type=F32)


def _dot_nt(a, b):
    return lax.dot_general(a, b, (((1,), (1,)), ((), ())), preferred_element_type=F32)


def _split3(x):
    hi = x.astype(BF16)
    r1 = x - hi.astype(F32)
    mid = r1.astype(BF16)
    lo = (r1 - mid.astype(F32)).astype(BF16)
    return hi, mid, lo


def _rms(x, g):
    return x * lax.rsqrt(jnp.mean(x * x, axis=-1, keepdims=True) + RMS_EPS) * g


def _gelu(x):
    return 0.5 * x * (1.0 + lax.erf(x * 0.7071067811865476))


def _sigmoid(x):
    return jax.nn.sigmoid(x)


def _const_spec(shape):
    nd = len(shape)
    return pl.BlockSpec(shape, lambda i: (0,) * nd, pipeline_mode=pl.Buffered(1))


def _layer_spec(stacked, layer):
    shape = stacked.shape[1:]
    return pl.BlockSpec((None,) + shape, lambda i: (layer,) + (0,) * len(shape), pipeline_mode=pl.Buffered(1))


def _row_spec(tm, width=D_MODEL):
    return pl.BlockSpec((tm, width), lambda i: (i, 0))


def _params():
    return pltpu.CompilerParams(dimension_semantics=("arbitrary",), vmem_limit_bytes=VMEM_LIMIT)


def _ffn_body(x, g, w1_ref, w2_ref):
    xn = _rms(x, g).astype(BF16)
    acc = x
    for c in range(FFN_HIDDEN // D_MODEL):
        sl = slice(c * D_MODEL, (c + 1) * D_MODEL)
        h = jnp.maximum(_dot(xn, w1_ref[:, sl]), 0.0)
        acc = acc + _dot((h * h).astype(BF16), w2_ref[sl, :])
    return acc


def _ffn_kernel(x_ref, g_ref, w1_ref, w2_ref, o_ref):
    o_ref[...] = _ffn_body(x_ref[...], g_ref[...], w1_ref, w2_ref)


def _ffn(x, g, w1, w2, layer):
    m = x.shape[0]
    return pl.pallas_call(
        _ffn_kernel,
        grid=(m // TM_FFN,),
        in_specs=[_row_spec(TM_FFN), _const_spec((1, D_MODEL)), _layer_spec(w1, layer), _layer_spec(w2, layer)],
        out_specs=_row_spec(TM_FFN),
        out_shape=jax.ShapeDtypeStruct((m, D_MODEL), F32),
        compiler_params=_params(),
        name="ffn",
    )(x, g, w1, w2)


def _sgu_kernel(x_ref, g_ref, win_ref, bin_ref, lng_ref, lnb_ref, ws_ref, bs_ref, wout_ref, g2_ref, w1_ref, w2_ref,
                o_ref, vn_ref, y_ref):
    x = x_ref[...]
    xn = _rms(x, g_ref[...]).astype(BF16)
    e = SGU_WIDTH
    zv = _gelu(_dot(xn, win_ref[:, e:]) + bin_ref[:, e:])
    mu = jnp.mean(zv, axis=-1, keepdims=True)
    var = jnp.mean(zv * zv, axis=-1, keepdims=True) - mu * mu
    vn_ref[...] = ((zv - mu) * lax.rsqrt(var + LN_EPS) * lng_ref[...] + lnb_ref[...]).astype(BF16)
    nblk = TM_SGU // SGU_BLOCK
    qi = lax.broadcasted_iota(jnp.int32, (SGU_BLOCK, SGU_BLOCK), 0) // CHUNK
    kj = lax.broadcasted_iota(jnp.int32, (SGU_BLOCK, SGU_BLOCK), 1) // CHUNK
    causal = kj <= qi
    gw = SGU_GROUP_DIM
    for gp in range(SGU_GROUPS // 2):
        cs2 = slice(2 * gp * gw, (2 * gp + 2) * gw)
        u2 = _gelu(_dot(xn, win_ref[:, cs2]) + bin_ref[:, cs2])
        for half in range(2):
            gi = 2 * gp + half
            cs = slice(gi * gw, (gi + 1) * gw)
            ws = jnp.where(causal, ws_ref[gi], 0.0).astype(BF16)
            rhs = jnp.concatenate([vn_ref[n * SGU_BLOCK:(n + 1) * SGU_BLOCK, cs] for n in range(nblk)], axis=1)
            s = _dot(ws, rhs) + bs_ref[gi]
            for n in range(nblk):
                rs = slice(n * SGU_BLOCK, (n + 1) * SGU_BLOCK)
                y_ref[rs, cs] = (u2[rs, half * gw:(half + 1) * gw] * s[:, n * gw:(n + 1) * gw]).astype(BF16)
    o_ref[...] = _ffn_body(x + _dot(y_ref[...], wout_ref[...]), g2_ref[...], w1_ref, w2_ref)


def _sgu(x, g, w_in, b_in, ln_g, ln_b, w_s, b_s, w_out, layer, g2, w1, w2, ffn_layer):
    m = x.shape[0]
    e = SGU_WIDTH
    return pl.pallas_call(
        _sgu_kernel,
        grid=(m // TM_SGU,),
        in_specs=[_row_spec(TM_SGU), _const_spec((1, D_MODEL)), _layer_spec(w_in, layer),
                  _const_spec((1, 2 * e)), _const_spec((1, e)), _const_spec((1, e)),
                  _const_spec((SGU_GROUPS, SGU_BLOCK, SGU_BLOCK)), _const_spec((SGU_GROUPS, SGU_BLOCK, 1)),
                  _layer_spec(w_out, layer), _const_spec((1, D_MODEL)), _layer_spec(w1, ffn_layer),
                  _layer_spec(w2, ffn_layer)],
        out_specs=_row_spec(TM_SGU),
        out_shape=jax.ShapeDtypeStruct((m, D_MODEL), F32),
        scratch_shapes=[pltpu.VMEM((TM_SGU, e), BF16), pltpu.VMEM((TM_SGU, e), BF16)],
        compiler_params=_params(),
        name="sgu_ffn",
    )(x, g, w_in, b_in, ln_g, ln_b, w_s, b_s, w_out, g2, w1, w2)


def _head_sum(x, ones_ref):
    tm, nt = x.shape[0], x.shape[1] // LANES
    stacked = jnp.concatenate([x[:, j * LANES:(j + 1) * LANES] for j in range(nt)], axis=0).astype(BF16)
    s = _dot(stacked, ones_ref[...])
    return jnp.concatenate([s[j * tm:(j + 1) * tm] for j in range(nt)], axis=1)


def _rwkv_in_kernel(*refs, tiles_per_seq, has_vmix):
    if has_vmix:
        (x_ref, xp_ref, vf_ref, g_ref, mu_ref, wrkv_ref, w0_ref, wla_ref, wlb_ref, a0_ref, ala_ref, alb_ref,
         v0_ref, vla_ref, vlb_ref, gla_ref, glb_ref, kk_ref, ka_ref, rk_ref, ones_ref,
         r_o, w_o, k_o, v_o, kk_o, b_o, g_o, bonus_o) = refs
    else:
        (x_ref, xp_ref, g_ref, mu_ref, wrkv_ref, w0_ref, wla_ref, wlb_ref, a0_ref, ala_ref, alb_ref,
         gla_ref, glb_ref, kk_ref, ka_ref, rk_ref, ones_ref,
         r_o, w_o, k_o, v_o, kk_o, b_o, g_o, bonus_o) = refs
    i = pl.program_id(0)
    gn = g_ref[...]
    h = _rms(x_ref[...], gn)
    hp = _rms(xp_ref[7:8, :], gn)
    hp = jnp.where(i % tiles_per_seq == 0, 0.0, hp)
    row = lax.broadcasted_iota(jnp.int32, h.shape, 0)
    hs = jnp.where(row == 0, hp, pltpu.roll(h, 1, 0))
    xx = hs - h

    def mix(p):
        return (h + xx * mu_ref[p:p + 1, :]).astype(BF16)

    xv = mix(2)
    tw = _dot(mix(3), wla_ref[...])
    ta = _dot(mix(4), ala_ref[...])
    tg = _dot(mix(5), gla_ref[...])
    if has_vmix:
        tv = _dot(xv, vla_ref[...])
    k = _dot(mix(1), wrkv_ref[1])
    r = _dot(mix(0), wrkv_ref[0])
    kk = k * kk_ref[...]
    ss = _head_sum(kk * kk, ones_ref)
    lw = w0_ref[...] + _dot(jnp.tanh(tw).astype(BF16), wlb_ref[...])
    a = _sigmoid(a0_ref[...] + _dot(ta.astype(BF16), alb_ref[...]))
    nrm = jnp.sqrt(ss)
    v = _dot(xv, wrkv_ref[2])
    r_o[...] = r
    k = k * (1.0 + (a - 1.0) * ka_ref[...])
    rk = _head_sum(r * k * rk_ref[...], ones_ref)
    g_o[...] = _dot(_sigmoid(tg).astype(BF16), glb_ref[...]).astype(BF16)
    if has_vmix:
        mixv = _sigmoid(v0_ref[...] + _dot(tv.astype(BF16), vlb_ref[...]))
        v = v + (vf_ref[...] - v) * mixv
    w_o[...] = -0.6065306597126334 * _sigmoid(lw)
    k_o[...] = k
    v_o[...] = v
    kk = kk / jnp.maximum(nrm, 1e-12)
    kk_o[...] = kk
    b_o[...] = kk * a
    bonus_o[...] = (rk * v).astype(BF16)


def _rwkv_in(x, v_first, p, t_len):
    m = x.shape[0]
    tm = TM_RWKV
    has_vmix = v_first is not None
    prev_spec = pl.BlockSpec((8, D_MODEL), lambda i: (jnp.maximum(i * (tm // 8) - 1, 0), 0))
    ins = [x, x] + ([v_first] if has_vmix else [])
    specs = [_row_spec(tm), prev_spec] + ([_row_spec(tm)] if has_vmix else [])

    def add(a, spec=None):
        ins.append(a)
        specs.append(spec if spec is not None else _const_spec(a.shape))

    add(p["g"]); add(p["mu"]); add(p["w_rkv"], _layer_spec(p["w_rkv"], p["layer"]))
    add(p["w0"]); add(p["w_la"]); add(p["w_lb"])
    add(p["a0"]); add(p["a_la"]); add(p["a_lb"])
    if has_vmix:
        add(p["v0"]); add(p["v_la"]); add(p["v_lb"])
    add(p["g_la"]); add(p["g_lb"]); add(p["k_k"]); add(p["k_a"]); add(p["r_k"]); add(p["ones"])
    out = jax.ShapeDtypeStruct((m, D_MODEL), F32)
    half = jax.ShapeDtypeStruct((m, D_MODEL), BF16)
    return pl.pallas_call(
        functools.partial(_rwkv_in_kernel, tiles_per_seq=t_len // tm, has_vmix=has_vmix),
        grid=(m // tm,),
        in_specs=specs,
        out_specs=[_row_spec(tm)] * 8,
        out_shape=[out] * 6 + [half] * 2,
        compiler_params=_params(),
        name="rwkv_in",
    )(*ins)


def _wkv_kernel(r_ref, w_ref, k_ref, v_ref, kk_ref, b_ref, y_ref, s_ref, *, nb, npairs):
    lc, hd, pw = WKV_CHUNK, HEAD_DIM, PAIR

    @pl.when(pl.program_id(0) == 0)
    def _():
        s_ref[...] = jnp.zeros_like(s_ref)

    lane = lax.broadcasted_iota(jnp.int32, (lc, pw), 1)
    row = lax.broadcasted_iota(jnp.int32, (lc, pw), 0)
    m0 = lane < hd
    col = jnp.where(m0, lane, lane - hd)
    strict = col < row
    incl = col <= row
    m0w = (lax.broadcasted_iota(jnp.int32, (lc, 2 * pw), 1) % pw) < hd
    tr = lax.broadcasted_iota(jnp.int32, (lc, 3 * lc), 0)
    tc = lax.broadcasted_iota(jnp.int32, (lc, 3 * lc), 1) % lc
    tri3 = jnp.where(tc <= tr, 1.0, 0.0).astype(BF16)

    def bdiag(x):
        z = jnp.zeros_like(x)
        return jnp.concatenate([jnp.where(m0, x, z), jnp.where(m0, z, x)], axis=0)

    tss = [slice(c * lc, (c + 1) * lc) for c in range(WKV_CHUNKS_PER_STEP)]

    def front(c):
        return _wkv_front(tss[c], r_ref, w_ref, k_ref, v_ref, kk_ref, b_ref, nb, npairs, strict, incl, tri3, bdiag)

    cur = front(0)
    for c in range(len(tss)):
        nxt = functools.partial(front, c + 1) if c + 1 < len(tss) else None
        cur = _wkv_state(tss[c], cur, y_ref, s_ref, npairs, m0, m0w, bdiag, nxt)


def _wkv_front(ts, r_ref, w_ref, k_ref, v_ref, kk_ref, b_ref, nb, npairs, strict, incl, tri3, bdiag):
    lc, pw = WKV_CHUNK, PAIR
    chains = []
    for bi in range(nb):
        w = w_ref[bi, ts, :]
        cum = _dot(tri3, jnp.concatenate(_split3(w), axis=0))
        tot = cum[lc - 1:lc, :]
        e_incl = jnp.exp(cum)
        e_excl = jnp.exp(cum - w)
        e_inv = jnp.exp(-cum)
        e_rem = jnp.exp(tot - cum)
        k = k_ref[bi, ts, :]
        b = b_ref[bi, ts, :]
        chains.extend((bi, p,
                       (-(kk_ref[bi, ts, :] * e_excl)).astype(BF16), (r_ref[bi, ts, :] * e_incl).astype(BF16),
                       (b * e_inv).astype(BF16), (k * e_inv).astype(BF16),
                       b * e_rem, k * e_rem,
                       v_ref[bi, ts, :].astype(BF16), jnp.exp(tot)) for p in range(npairs))

    n = len(chains)
    sls = [slice(c[1] * pw, (c[1] + 1) * pw) for c in chains]
    at, rt, bt, kt, bh, kh, vp, pt = ([c[j][:, sl] for c, sl in zip(chains, sls)] for j in range(2, 10))
    ar = [jnp.concatenate([at[i], rt[i]], axis=0) for i in range(n)]
    g = [_dot_nt(ar[i], jnp.concatenate([bdiag(bt[i]), bdiag(kt[i])], axis=0)) for i in range(n)]
    nmat = [jnp.where(strict, g[i][:lc, :pw], 0.0) for i in range(n)]
    aak = [jnp.where(strict, g[i][:lc, pw:], 0.0).astype(BF16) for i in range(n)]
    arbk = [jnp.concatenate([jnp.where(incl, g[i][lc:, :pw], 0.0), jnp.where(incl, g[i][lc:, pw:], 0.0)],
                            axis=1).astype(BF16) for i in range(n)]
    vbd = [bdiag(vp[i]) for i in range(n)]
    return dict(chains=chains, sls=sls, at=at, rt=rt, bh=bh, kh=kh, pt=pt, nmat=nmat, aak=aak, arbk=arbk, vbd=vbd)


def _wkv_state(ts, fr, y_ref, s_ref, npairs, m0, m0w, bdiag, trace_next):
    lc, hd, pw = WKV_CHUNK, HEAD_DIM, PAIR
    chains, sls, at, rt, bh, kh, pt, nmat, aak, arbk, vbd = (fr[q] for q in (
        "chains", "sls", "at", "rt", "bh", "kh", "pt", "nmat", "aak", "arbk", "vbd"))
    n = len(chains)
    sidx = [c[0] * npairs + c[1] for c in chains]
    s = [s_ref[sidx[i]] for i in range(n)]
    sbd = [bdiag(s[i].astype(BF16)) for i in range(n)]
    wmat = [_dot(jnp.concatenate([at[i], aak[i]], axis=1), jnp.concatenate([sbd[i], vbd[i]], axis=0))
            for i in range(n)]
    rs = [_dot(rt[i], sbd[i]) for i in range(n)]
    nxt = trace_next() if trace_next is not None else None
    for it in range(6):
        if it < 5:
            o = []
            for i in range(n):
                pm = jnp.concatenate([nmat[i], wmat[i]], axis=1).astype(BF16)
                z = jnp.zeros_like(pm)
                rhs = jnp.concatenate([jnp.where(m0w, pm, z), jnp.where(m0w, z, pm)], axis=0)
                o.append(_dot(nmat[i].astype(BF16), rhs))
            wmat = [wmat[i] + o[i][:, pw:] for i in range(n)]
            nmat = [o[i][:, :pw] for i in range(n)]
        else:
            o = [_dot(nmat[i].astype(BF16), bdiag(wmat[i].astype(BF16))) for i in range(n)]
            wmat = [wmat[i] + o[i] for i in range(n)]
    u16 = [wmat[i].astype(BF16) for i in range(n)]
    y = [rs[i] + _dot(arbk[i], jnp.concatenate([bdiag(u16[i]), vbd[i]], axis=0)) for i in range(n)]
    upd = []
    pcol = []
    for i in range(n):
        ubd = bdiag(u16[i])
        tk = jnp.transpose(jnp.concatenate([bh[i], kh[i]], axis=0)).astype(BF16)
        lhs = jnp.concatenate([tk[:hd], tk[hd:]], axis=1)
        upd.append(_dot(lhs, jnp.concatenate([ubd[:lc], vbd[i][:lc], ubd[lc:], vbd[i][lc:]], axis=0)))
        tp = jnp.transpose(jnp.broadcast_to(pt[i], (pw, pw)))
        pcol.append(jnp.where(m0, tp[:hd], tp[hd:]))
    for i in range(n):
        y_ref[chains[i][0], ts, sls[i]] = y[i]
        s_ref[sidx[i]] = s[i] * pcol[i] + upd[i]
    return nxt


def _wkv(r, w, k, v, kk, b):
    bsz, t_len, d = r.shape
    npairs = d // PAIR
    t_blk = WKV_CHUNKS_PER_STEP * WKV_CHUNK
    spec = pl.BlockSpec((bsz, t_blk, d), lambda t: (0, t, 0))
    return pl.pallas_call(
        functools.partial(_wkv_kernel, nb=bsz, npairs=npairs),
        grid=(t_len // t_blk,),
        in_specs=[spec] * 6,
        out_specs=spec,
        out_shape=jax.ShapeDtypeStruct((bsz, t_len, d), F32),
        scratch_shapes=[pltpu.VMEM((bsz * npairs, HEAD_DIM, PAIR), F32)],
        compiler_params=_params(),
        name="wkv7",
    )(r, w, k, v, kk, b)


def _rwkv_out_kernel(y_ref, bonus_ref, gate_ref, x_ref, lng_ref, lnb_ref, wo_ref, ones_ref,
                     g_ref, w1_ref, w2_ref, fg_ref, o_ref, *, final_norm):
    hr = y_ref.shape[0] // 2
    rsl = [slice(i * hr, (i + 1) * hr) for i in range(2)]
    y = [y_ref[r, :] for r in rsl]
    mu = [_head_sum(v, ones_ref) * (1.0 / HEAD_DIM) for v in y]
    yc = [a - b for a, b in zip(y, mu)]
    var = [_head_sum(v * v, ones_ref) * (1.0 / HEAD_DIM) for v in yc]
    yn = [a * lax.rsqrt(b + GN_EPS) * lng_ref[...] + lnb_ref[...] for a, b in zip(yc, var)]
    out = [((a + bonus_ref[r, :]) * gate_ref[r, :]).astype(BF16) for a, r in zip(yn, rsl)]
    x = jnp.concatenate([x_ref[r, :] + _dot(a, wo_ref[...]) for a, r in zip(out, rsl)], axis=0)
    res = _ffn_body(x, g_ref[...], w1_ref, w2_ref)
    if final_norm:
        res = _rms(res, fg_ref[...])
    o_ref[...] = res


def _rwkv_out(y, bonus, gate, x, p, g, w1, w2, ffn_layer, fg, final_norm):
    m = x.shape[0]
    tm = TM_OUT
    vec = _const_spec((1, D_MODEL))
    return pl.pallas_call(
        functools.partial(_rwkv_out_kernel, final_norm=final_norm),
        grid=(m // tm,),
        in_specs=[_row_spec(tm)] * 4 + [vec, vec, _layer_spec(p["w_o"], p["layer"]),
                                        _const_spec(p["ones"].shape), vec,
                                        _layer_spec(w1, ffn_layer), _layer_spec(w2, ffn_layer), vec],
        out_specs=_row_spec(tm),
        out_shape=jax.ShapeDtypeStruct((m, D_MODEL), F32),
        compiler_params=_params(),
        name="rwkv_out_ffn",
    )(y, bonus, gate, x, p["ln_g"], p["ln_b"], p["w_o"], p["ones"], g, w1, w2, fg)


def kernel(x, norm_mix_g, norm_ffn_g, final_norm_g, ffn_w1, ffn_w2, sgu_w_in, sgu_b_in, sgu_ln_g, sgu_ln_b, sgu_w_s, sgu_b_s, sgu_w_out, rwkv_mu, rwkv_w_rkv, rwkv_w0, rwkv_w_lora_a, rwkv_w_lora_b, rwkv_a0, rwkv_a_lora_a, rwkv_a_lora_b, rwkv_v0, rwkv_v_lora_a, rwkv_v_lora_b, rwkv_g_lora_a, rwkv_g_lora_b, rwkv_k_k, rwkv_k_a, rwkv_r_k, rwkv_ln_g, rwkv_ln_b, rwkv_w_o):
    bsz, t_len, d = x.shape
    m = bsz * t_len
    row = lambda a: a.reshape(1, -1)
    head_of_lane = jnp.arange(LANES) // HEAD_DIM
    ones = (head_of_lane[:, None] == head_of_lane[None, :]).astype(BF16)
    xf = x.reshape(m, d)
    w1, w2 = ffn_w1.astype(BF16), ffn_w2.astype(BF16)
    w_in, w_out = sgu_w_in.astype(BF16), sgu_w_out.astype(BF16)
    w_rkv, w_o = rwkv_w_rkv.astype(BF16), rwkv_w_o.astype(BF16)
    v_first = None
    for i in range(DEPTH):
        j = i // 2
        if i % 2 == 0:
            xf = _sgu(xf, row(norm_mix_g[i]), w_in, row(sgu_b_in[j]), row(sgu_ln_g[j]),
                      row(sgu_ln_b[j]), sgu_w_s[j], sgu_b_s[j][:, :, None], w_out, j,
                      row(norm_ffn_g[i]), w1, w2, i)
        else:
            p = dict(g=row(norm_mix_g[i]), mu=rwkv_mu[j], w_rkv=w_rkv, layer=j, w0=row(rwkv_w0[j]),
                     w_la=rwkv_w_lora_a[j].astype(BF16), w_lb=rwkv_w_lora_b[j].astype(BF16),
                     a0=row(rwkv_a0[j]), a_la=rwkv_a_lora_a[j].astype(BF16), a_lb=rwkv_a_lora_b[j].astype(BF16),
                     g_la=rwkv_g_lora_a[j].astype(BF16), g_lb=rwkv_g_lora_b[j].astype(BF16),
                     k_k=row(rwkv_k_k[j]), k_a=row(rwkv_k_a[j]), r_k=row(rwkv_r_k[j]),
                     ln_g=row(rwkv_ln_g[j]), ln_b=row(rwkv_ln_b[j]), w_o=w_o,
                     ones=ones)
            if j > 0:
                p.update(v0=row(rwkv_v0[j - 1]), v_la=rwkv_v_lora_a[j - 1].astype(BF16),
                         v_lb=rwkv_v_lora_b[j - 1].astype(BF16))
            r, w, k, v, kk, b, gate, bonus = _rwkv_in(xf, v_first if j > 0 else None, p, t_len)
            if j == 0:
                v_first = v
            sh = (bsz, t_len, d)
            y = _wkv(r.reshape(sh), w.reshape(sh), k.reshape(sh), v.reshape(sh), kk.reshape(sh), b.reshape(sh))
            xf = _rwkv_out(y.reshape(m, d), bonus, gate, xf, p, row(norm_ffn_g[i]), w1, w2, i, row(final_norm_g),
                           i == DEPTH - 1)
    return xf.reshape(bsz, t_len, d)
```
